```python
import math
import jax, jax.numpy as jnp
from jax import lax
import numpy as np

D_MODEL = 1024
BATCH = 1
SEQ = 16384
DEPTH = 1

CHUNK = 64
MIX_WIDTH = D_MODEL
ATTN_WIDTH = MIX_WIDTH // 2
LRU_WIDTH = MIX_WIDTH - ATTN_WIDTH
ATTN_HEAD_DIM = 64
ATTN_HEADS = ATTN_WIDTH // (2 * ATTN_HEAD_DIM)
LRU_BLOCKS = 8
LRU_BLOCK_DIM = LRU_WIDTH // LRU_BLOCKS
CONV_WIDTH = 4
LRU_C = 8.0
Q_BLOCK = 128
IN_COLS = 3 * ATTN_WIDTH + 2 * LRU_WIDTH
PEER_HEADS = 8
PEER_N_KEYS = 128
PEER_N_EXPERTS = PEER_N_KEYS * PEER_N_KEYS
PEER_QUERY_DIM = 256
PEER_HALF = PEER_QUERY_DIM // 2
PEER_TOPK = 16
PEER_BLOCK = 128
EPS = 1e-6

kernel_name = "hybrid_diffattn_rglru_peer"


def rmsnorm(x, g):
    x32 = x.astype(jnp.float32)
    y = x32 * lax.rsqrt(jnp.mean(x32 * x32, axis=-1, keepdims=True) + EPS)
    return (y * g.astype(jnp.float32)).astype(x.dtype)


def diff_attention(q, k, v, lam, subln_g, lambda_init):
    B, S = q.shape[0], q.shape[1]
    nb = S // Q_BLOCK
    scale = ATTN_HEAD_DIM ** -0.5
    k32 = k.astype(jnp.float32)
    v32 = v.astype(jnp.float32)
    k_chunk = jnp.arange(S) // CHUNK
    qb = q.reshape(B, nb, Q_BLOCK, ATTN_HEADS, 2, ATTN_HEAD_DIM).swapaxes(0, 1)

    def block(args):
        qblk, i = args
        q_chunk = (i * Q_BLOCK + jnp.arange(Q_BLOCK)) // CHUNK
        s = jnp.einsum('bqhmd,bkhmd->bhmqk', qblk.astype(jnp.float32), k32) * scale
        mask = q_chunk[:, None] >= k_chunk[None, :]
        s = jnp.where(mask, s, -1e30)
        p = jax.nn.softmax(s, axis=-1)
        p = p[:, :, 0] - lam * p[:, :, 1]
        return jnp.einsum('bhqk,bkhe->bqhe', p, v32)

    o = lax.map(block, (qb, jnp.arange(nb)))
    o = o.swapaxes(0, 1).reshape(B, S, ATTN_HEADS, 2 * ATTN_HEAD_DIM)
    o32 = o * lax.rsqrt(jnp.mean(o * o, axis=-1, keepdims=True) + EPS)
    o32 = o32 * subln_g.astype(jnp.float32) * (1.0 - lambda_init)
    return o32.reshape(B, S, ATTN_WIDTH).astype(q.dtype)


def rglru_branch(xr, gate, conv_w, conv_b, w_a, b_a, w_x, b_x, lru_lambda):
    B, S, W = xr.shape
    xp = jnp.pad(xr, ((0, 0), (CONV_WIDTH - 1, 0), (0, 0)))
    y = conv_b
    for t in range(CONV_WIDTH):
        y = y + xp[:, t:t + S, :] * conv_w[t]
    yb = y.reshape(B, S, LRU_BLOCKS, LRU_BLOCK_DIM)
    r = jax.nn.sigmoid((jnp.einsum('bsnd,nde->bsne', yb, w_a).reshape(B, S, W) + b_a).astype(jnp.float32))
    i = jax.nn.sigmoid((jnp.einsum('bsnd,nde->bsne', yb, w_x).reshape(B, S, W) + b_x).astype(jnp.float32))
    log_a = -LRU_C * r * jax.nn.softplus(-lru_lambda.astype(jnp.float32))
    a = jnp.exp(log_a)
    mult = jnp.sqrt(jnp.maximum(-jnp.expm1(2.0 * log_a), 1e-12))
    b = mult * (i * y.astype(jnp.float32))

    def combine(left, right):
        a1, b1 = left
        a2, b2 = right
        return a1 * a2, a2 * b1 + b2

    _, h = lax.associative_scan(combine, (a, b), axis=1)
    out = h * jax.nn.gelu(gate.astype(jnp.float32))
    return out.astype(xr.dtype)


def peer_ffn(xn, w_query, sub_keys_1, sub_keys_2, expert_down, expert_up):
    B, S, D = xn.shape
    T = B * S
    xt = xn.reshape(T, D)
    q = (xt @ w_query).reshape(T, PEER_HEADS, 2, PEER_HALF).astype(jnp.float32)
    s1 = jnp.einsum('thd,kd->thk', q[:, :, 0], sub_keys_1.astype(jnp.float32))
    s2 = jnp.einsum('thd,kd->thk', q[:, :, 1], sub_keys_2.astype(jnp.float32))
    v1, i1 = lax.top_k(s1, PEER_TOPK)
    v2, i2 = lax.top_k(s2, PEER_TOPK)
    cand = (v1[..., :, None] + v2[..., None, :]).reshape(T, PEER_HEADS, PEER_TOPK * PEER_TOPK)
    sc, ci = lax.top_k(cand, PEER_TOPK)
    eid = (jnp.take_along_axis(i1, ci // PEER_TOPK, axis=-1) * PEER_N_KEYS
           + jnp.take_along_axis(i2, ci % PEER_TOPK, axis=-1))
    g = jax.nn.softmax(sc, axis=-1)
    nbt = T // PEER_BLOCK

    def block(args):
        xb, eb, gb = args
        u = expert_down[eb]
        act = jax.nn.gelu(jnp.einsum('thkd,td->thk', u, xb).astype(jnp.float32))
        w = (gb * act).astype(xb.dtype)
        return jnp.einsum('thk,thkd->td', w, expert_up[eb])

    out = lax.map(block, (xt.reshape(nbt, PEER_BLOCK, D),
                          eid.reshape(nbt, PEER_BLOCK, PEER_HEADS, PEER_TOPK),
                          g.reshape(nbt, PEER_BLOCK, PEER_HEADS, PEER_TOPK)))
    return out.reshape(B, S, D).astype(xn.dtype)


def setup_inputs(seed: int = 0) -> dict:
    key = jax.random.key(seed)
    ks = jax.random.split(key, 24)
    f32 = jnp.float32
    L, D = DEPTH, D_MODEL
    nrm = lambda k, shape, s: jax.random.normal(k, shape, f32) * s
    a8 = jax.random.uniform(ks[14], (L, LRU_WIDTH), f32, 0.9, 0.999)
    a = a8 ** (1.0 / LRU_C)
    return {
        "x": jax.random.normal(ks[0], (BATCH, SEQ, D), f32),
        "norm1_g": 1.0 + nrm(ks[1], (L, D), 0.02),
        "w_in": nrm(ks[2], (L, D, IN_COLS), D ** -0.5),
        "lambda_q1": nrm(ks[3], (L, ATTN_HEAD_DIM), 0.1),
        "lambda_k1": nrm(ks[4], (L, ATTN_HEAD_DIM), 0.1),
        "lambda_q2": nrm(ks[5], (L, ATTN_HEAD_DIM), 0.1),
        "lambda_k2": nrm(ks[6], (L, ATTN_HEAD_DIM), 0.1),
        "subln_g": 1.0 + nrm(ks[7], (L, 2 * ATTN_HEAD_DIM), 0.02),
        "conv_w": nrm(ks[8], (L, CONV_WIDTH, LRU_WIDTH), CONV_WIDTH ** -0.5),
        "conv_b": nrm(ks[9], (L, LRU_WIDTH), 0.01),
        "w_rec_gate": nrm(ks[10], (L, LRU_BLOCKS, LRU_BLOCK_DIM, LRU_BLOCK_DIM), LRU_BLOCK_DIM ** -0.5),
        "b_rec_gate": nrm(ks[11], (L, LRU_WIDTH), 0.01),
        "w_in_gate": nrm(ks[12], (L, LRU_BLOCKS, LRU_BLOCK_DIM, LRU_BLOCK_DIM), LRU_BLOCK_DIM ** -0.5),
        "b_in_gate": nrm(ks[13], (L, LRU_WIDTH), 0.01),
        "lru_lambda": jnp.log(a) - jnp.log1p(-a),
        "w_out": nrm(ks[15], (L, MIX_WIDTH, D), MIX_WIDTH ** -0.5),
        "norm2_g": 1.0 + nrm(ks[16], (L, D), 0.02),
        "w_query": nrm(ks[17], (L, D, PEER_HEADS * PEER_QUERY_DIM), D ** -0.5),
        "sub_keys_1": nrm(ks[18], (L, PEER_N_KEYS, PEER_HALF), PEER_HALF ** -0.5),
        "sub_keys_2": nrm(ks[19], (L, PEER_N_KEYS, PEER_HALF), PEER_HALF ** -0.5),
        "expert_down": nrm(ks[20], (L, PEER_N_EXPERTS, D), D ** -0.5),
        "expert_up": nrm(ks[21], (L, PEER_N_EXPERTS, D), (PEER_HEADS * PEER_TOPK) ** -0.5),
        "norm_f_g": 1.0 + nrm(ks[22], (D,), 0.02),
    }


def reference(x, norm1_g, w_in, lambda_q1, lambda_k1, lambda_q2, lambda_k2, subln_g,
              conv_w, conv_b, w_rec_gate, b_rec_gate, w_in_gate, b_in_gate, lru_lambda,
              w_out, norm2_g, w_query, sub_keys_1, sub_keys_2, expert_down, expert_up,
              norm_f_g):
    B, S, D = x.shape
    h = x
    for l in range(DEPTH):
        lambda_init = 0.8 - 0.6 * math.exp(-0.3 * l)
        n = rmsnorm(h, norm1_g[l])
        p = n @ w_in[l]
        q = p[..., :ATTN_WIDTH].reshape(B, S, ATTN_HEADS, 2, ATTN_HEAD_DIM)
        k = p[..., ATTN_WIDTH:2 * ATTN_WIDTH].reshape(B, S, ATTN_HEADS, 2, ATTN_HEAD_DIM)
        v = p[..., 2 * ATTN_WIDTH:3 * ATTN_WIDTH].reshape(B, S, ATTN_HEADS, 2 * ATTN_HEAD_DIM)
        xr = p[..., 3 * ATTN_WIDTH:3 * ATTN_WIDTH + LRU_WIDTH]
        gate = p[..., 3 * ATTN_WIDTH + LRU_WIDTH:]
        lam = (jnp.exp(jnp.sum(lambda_q1[l].astype(jnp.float32) * lambda_k1[l].astype(jnp.float32)))
               - jnp.exp(jnp.sum(lambda_q2[l].astype(jnp.float32) * lambda_k2[l].astype(jnp.float32)))
               + lambda_init)
        attn_out = diff_attention(q, k, v, lam, subln_g[l], lambda_init)
        lru_out = rglru_branch(xr, gate, conv_w[l], conv_b[l], w_rec_gate[l], b_rec_gate[l],
                               w_in_gate[l], b_in_gate[l], lru_lambda[l])
        mix = jnp.concatenate([attn_out, lru_out], axis=-1)
        h = h + mix @ w_out[l]
        h = h + peer_ffn(rmsnorm(h, norm2_g[l]), w_query[l], sub_keys_1[l], sub_keys_2[l],
                         expert_down[l], expert_up[l])
    return rmsnorm(h, norm_f_g)
```

```python
import functools
import math

import jax
import jax.numpy as jnp
from jax import lax
from jax.experimental import pallas as pl
from jax.experimental.pallas import tpu as pltpu

CHUNK = 64
ATTN_HEAD_DIM = 64
ATTN_HEADS = 4
ATTN_WIDTH = 2 * ATTN_HEAD_DIM * ATTN_HEADS
LRU_BLOCKS = 8
CONV_WIDTH = 4
LRU_C = 8.0
PEER_HEADS = 8
PEER_N_KEYS = 128
PEER_HALF = 128
PEER_TOPK = 16
EPS = 1e-6
NEG_INF = -1e30

LANES = 128
SUBLANES = 8
VMEM_LIMIT_BYTES = 48 * 1024 * 1024

_F32 = jnp.float32
_BF16 = jnp.bfloat16


def _cparams(*sem):
    return pltpu.CompilerParams(dimension_semantics=sem, vmem_limit_bytes=VMEM_LIMIT_BYTES)


def _rms(x, g):
    return x * lax.rsqrt(jnp.mean(x * x, axis=-1, keepdims=True) + EPS) * g


def _in_proj_kernel(x_ref, g_ref, w_ref, q_ref, k_ref, v_ref, xr_ref, gate_ref):
    n = _rms(x_ref[...], g_ref[...]).astype(_BF16)
    hw = 2 * ATTN_HEAD_DIM

    def proj(c0, width):
        return jnp.dot(n, w_ref[:, c0:c0 + width], preferred_element_type=_F32)

    scale = ATTN_HEAD_DIM ** -0.5
    for h in range(ATTN_HEADS):
        q_ref[h] = (proj(h * hw, hw) * scale).astype(_BF16)
        k_ref[h] = proj(ATTN_WIDTH + h * hw, hw).astype(_BF16)
        v_ref[h] = proj(2 * ATTN_WIDTH + h * hw, hw).astype(_BF16)
    lru_w = xr_ref.shape[-1]
    xr_ref[...] = proj(3 * ATTN_WIDTH, lru_w)
    gate_ref[...] = proj(3 * ATTN_WIDTH + lru_w, lru_w)


def _in_proj(x2, g, w_bf16, tm):
    T, D = x2.shape
    lru_w = (w_bf16.shape[1] - 3 * ATTN_WIDTH) // 2
    hw = 2 * ATTN_HEAD_DIM
    head_shape = jax.ShapeDtypeStruct((ATTN_HEADS, T, hw), _BF16)
    head_spec = pl.BlockSpec((ATTN_HEADS, tm, hw), lambda i: (0, i, 0))
    row_spec = pl.BlockSpec((tm, lru_w), lambda i: (i, 0))
    return pl.pallas_call(
        _in_proj_kernel,
        grid=(T // tm,),
        in_specs=[pl.BlockSpec((tm, D), lambda i: (i, 0)),
                  pl.BlockSpec((1, D), lambda i: (0, 0)),
                  pl.BlockSpec(w_bf16.shape, lambda i: (0, 0))],
        out_specs=[head_spec, head_spec, head_spec, row_spec, row_spec],
        out_shape=[head_shape, head_shape, head_shape,
                   jax.ShapeDtypeStruct((T, lru_w), _F32), jax.ShapeDtypeStruct((T, lru_w), _F32)],
        compiler_params=_cparams("parallel"),
        name="in_proj",
    )(x2, g, w_bf16)


def _attn_kernel(lq1_ref, lk1_ref, lq2_ref, lk2_ref, sg_ref, q_ref, k_ref, v_ref, o_ref,
                 m_ref, l_ref, acc_ref, *, blk, lambda_init):
    i = pl.program_id(1)
    hd = ATTN_HEAD_DIM
    q = q_ref[...]
    lane = lax.broadcasted_iota(jnp.int32, q.shape, 1)
    zero = jnp.zeros_like(q)
    q_maps = (jnp.where(lane < hd, q, zero), jnp.where(lane >= hd, q, zero))

    m_ref[...] = jnp.full(m_ref.shape, NEG_INF, _F32)
    l_ref[...] = jnp.zeros(l_ref.shape, _F32)
    acc_ref[...] = jnp.zeros(acc_ref.shape, _F32)

    def block_update(j, masked):
        row0 = pl.multiple_of(j * blk, blk)
        kj = k_ref[pl.ds(row0, blk), :]
        vj = v_ref[pl.ds(row0, blk), :]
        if masked:
            qc = lax.broadcasted_iota(jnp.int32, (blk, blk), 0) // CHUNK
            kc = lax.broadcasted_iota(jnp.int32, (blk, blk), 1) // CHUNK
            visible = qc >= kc
        for mp in range(2):
            s = lax.dot_general(q_maps[mp], kj, (((1,), (1,)), ((), ())),
                                preferred_element_type=_F32)
            if masked:
                s = jnp.where(visible, s, NEG_INF)
            m_prev = m_ref[mp]
            m_next = jnp.maximum(m_prev, jnp.max(s, axis=1, keepdims=True))
            alpha = jnp.exp(m_prev - m_next)
            e = jnp.exp(s - jnp.tile(m_next, (1, blk // LANES)))
            l_ref[mp] = alpha * l_ref[mp] + jnp.sum(e, axis=1, keepdims=True)
            m_ref[mp] = m_next
            pv = jnp.dot(e.astype(_BF16), vj, preferred_element_type=_F32)
            acc_ref[mp] = alpha * acc_ref[mp] + pv

    def body(j, carry):
        block_update(j, masked=False)
        return carry

    lax.fori_loop(0, i, body, 0)
    block_update(i, masked=True)

    lam = (jnp.exp(jnp.sum(lq1_ref[...] * lk1_ref[...], keepdims=True))
           - jnp.exp(jnp.sum(lq2_ref[...] * lk2_ref[...], keepdims=True)) + lambda_init)
    o = acc_ref[0] / l_ref[0] - lam * (acc_ref[1] / l_ref[1])
    o = o * lax.rsqrt(jnp.mean(o * o, axis=-1, keepdims=True) + EPS)
    o_ref[...] = (o * sg_ref[...] * (1.0 - lambda_init)).astype(o_ref.dtype)


def _attention(q, k, v, lq1, lk1, lq2, lk2, subln_g, lambda_init, blk):
    H, T, hw = q.shape
    vec = pl.BlockSpec((1, ATTN_HEAD_DIM), lambda h, i: (0, 0))
    full = pl.BlockSpec((None, T, hw), lambda h, i: (h, 0, 0))
    kernel = functools.partial(_attn_kernel, blk=blk, lambda_init=lambda_init)
    return pl.pallas_call(
        kernel,
        grid=(H, T // blk),
        in_specs=[vec, vec, vec, vec,
                  pl.BlockSpec((1, hw), lambda h, i: (0, 0)),
                  pl.BlockSpec((None, blk, hw), lambda h, i: (h, i, 0)),
                  full, full],
        out_specs=pl.BlockSpec((blk, hw), lambda h, i: (i, h)),
        out_shape=jax.ShapeDtypeStruct((T, H * hw), _BF16),
        scratch_shapes=[pltpu.VMEM((2, blk, LANES), _F32),
                        pltpu.VMEM((2, blk, LANES), _F32),
                        pltpu.VMEM((2, blk, hw), _F32)],
        compiler_params=_cparams("parallel", "parallel"),
        name="diff_attention",
    )(lq1, lk1, lq2, lk2, subln_g, q, k, v)


def _shift_rows(x, s, fill):
    rolled = pltpu.roll(x, s, axis=0)
    row = lax.broadcasted_iota(jnp.int32, x.shape, 0)
    return jnp.where(row >= s, rolled, fill)


def _rglru_kernel(xr_ref, gate_ref, cw_ref, cb_ref, wg_ref, ba_ref, bx_ref, lam_ref, o_ref,
                  xbuf_ref, hprev_ref, *, tl):
    halo = SUBLANES
    W = xr_ref.shape[-1]

    @pl.when(pl.program_id(0) == 0)
    def _():
        xbuf_ref[0:halo, :] = jnp.zeros((halo, W), _F32)
        hprev_ref[...] = jnp.zeros(hprev_ref.shape, _F32)

    x = xr_ref[...]
    xbuf_ref[halo:halo + tl, :] = x
    y = cb_ref[...] + x * cw_ref[CONV_WIDTH - 1:CONV_WIDTH, :]
    for t in range(CONV_WIDTH - 1):
        back = CONV_WIDTH - 1 - t
        y = y + xbuf_ref[halo - back:halo - back + tl, :] * cw_ref[t:t + 1, :]
    xbuf_ref[0:halo, :] = x[tl - halo:, :]

    gates = jnp.dot(y.astype(_BF16), wg_ref[...], preferred_element_type=_F32)
    r = jax.nn.sigmoid(gates[:, :W] + ba_ref[...])
    ig = jax.nn.sigmoid(gates[:, W:] + bx_ref[...])
    log_a = -LRU_C * r * jax.nn.softplus(-lam_ref[...])
    a = jnp.exp(log_a)
    one_minus_a2 = -jnp.tanh(log_a) * (1.0 + a * a)
    b = jnp.sqrt(jnp.maximum(one_minus_a2, 1e-12)) * (ig * y)

    s = 1
    while s < tl:
        a_sh = _shift_rows(a, s, 1.0)
        b_sh = _shift_rows(b, s, 0.0)
        b = a * b_sh + b
        a = a * a_sh
        s *= 2
    h = b + a * hprev_ref[0:1, :]
    hprev_ref[...] = jnp.broadcast_to(h[tl - 1:tl, :], hprev_ref.shape)
    o_ref[...] = (h * jax.nn.gelu(gate_ref[...])).astype(o_ref.dtype)


def _rglru(xr, gate, conv_w, conv_b, wg_bf16, b_a, b_x, lru_lambda, tl):
    T, W = xr.shape
    row = pl.BlockSpec((tl, W), lambda i: (i, 0))
    vec = pl.BlockSpec((1, W), lambda i: (0, 0))
    return pl.pallas_call(
        functools.partial(_rglru_kernel, tl=tl),
        grid=(T // tl,),
        in_specs=[row, row,
                  pl.BlockSpec(conv_w.shape, lambda i: (0, 0)),
                  vec,
                  pl.BlockSpec(wg_bf16.shape, lambda i: (0, 0)),
                  vec, vec, vec],
        out_specs=row,
        out_shape=jax.ShapeDtypeStruct((T, W), _BF16),
        scratch_shapes=[pltpu.VMEM((tl + SUBLANES, W), _F32),
                        pltpu.VMEM((SUBLANES, W), _F32)],
        compiler_params=_cparams("arbitrary"),
        name="rglru",
    )(xr, gate, conv_w, conv_b, wg_bf16, b_a, b_x, lru_lambda)


def _oddeven_mergesort_pairs(n):
    pairs = []
    p = 1
    while p < n:
        k = p
        while k >= 1:
            for j in range(k % p, n - k, 2 * k):
                for i in range(min(k, n - j - k)):
                    if (i + j) // (2 * p) == (i + j + k) // (2 * p):
                        pairs.append((i + j, i + j + k))
            k //= 2
        p *= 2
    return pairs


_SORT16 = _oddeven_mergesort_pairs(PEER_TOPK)


def _sort_desc(xs):
    xs = list(xs)
    for lo, hi in _SORT16:
        a, b = xs[lo], xs[hi]
        xs[lo], xs[hi] = jnp.maximum(a, b), jnp.minimum(a, b)
    return xs


def _top_of_union(a_sorted, b_sorted):
    n = len(a_sorted)
    return [jnp.maximum(a_sorted[i], b_sorted[n - 1 - i]) for i in range(n)]


def _bitonic_merge_desc(xs):
    xs = list(xs)
    n = len(xs)
    d = n // 2
    while d >= 1:
        for i in range(n):
            if i & d == 0:
                a, b = xs[i], xs[i + d]
                xs[i], xs[i + d] = jnp.maximum(a, b), jnp.minimum(a, b)
        d //= 2
    return xs


def _top16_rows(s):
    slabs = [s[j * SUBLANES:(j + 1) * SUBLANES, :] for j in range(s.shape[0] // SUBLANES)]
    top = _sort_desc(slabs)
    for shift in (4, 2, 1):
        other = [pltpu.roll(t, shift, axis=0) for t in top]
        top = _bitonic_merge_desc(_top_of_union(top, other))
    return top


def _pair_threshold(t1, t2):
    k = PEER_TOPK
    rows = [[t1[a] + t2[b] for b in range(k // (a + 1))] for a in range(k)]
    neg = jnp.full_like(t1[0], NEG_INF)
    g1 = _sort_desc(rows[1] + rows[2] + rows[4])
    g2 = _sort_desc(rows[3] + rows[5] + rows[6] + rows[7]
                    + [rows[a][0] for a in range(8, 14)])
    g3 = _sort_desc([rows[14][0], rows[15][0]] + [neg] * (k - 2))
    top = _bitonic_merge_desc(_top_of_union(rows[0], g1))
    top = _bitonic_merge_desc(_top_of_union(top, g2))
    top = _top_of_union(top, g3)
    return top, rows[0][0]


def _mid_kernel(attn_ref, lru_ref, x_ref, wo_ref, g2_ref, wqt_ref, k1_ref, k2_ref,
                h_ref, xnt_ref, c1_ref, e1_ref, r2_ref, e2_ref, qt_ref):
    aw = attn_ref.shape[-1]
    h = (x_ref[...]
         + jnp.dot(attn_ref[...], wo_ref[0:aw, :], preferred_element_type=_F32)
         + jnp.dot(lru_ref[...], wo_ref[aw:, :], preferred_element_type=_F32))
    h_ref[...] = h
    xnt = jnp.transpose(_rms(h, g2_ref[...])).astype(_BF16)
    xnt_ref[...] = xnt
    qt_ref[...] = jnp.dot(wqt_ref[...], xnt, preferred_element_type=_F32)

    nslab = PEER_N_KEYS // SUBLANES

    def per_head(hd, carry):
        base = pl.multiple_of(hd * (2 * PEER_HALF), 2 * PEER_HALF)
        q1 = qt_ref[pl.ds(base, PEER_HALF), :].astype(_BF16)
        q2 = qt_ref[pl.ds(base + PEER_HALF, PEER_HALF), :].astype(_BF16)
        s1 = jnp.dot(k1_ref[...], q1, preferred_element_type=_F32)
        s2 = jnp.dot(k2_ref[...], q2, preferred_element_type=_F32)
        t1 = _top16_rows(s1)
        t2 = _top16_rows(s2)
        top, m = _pair_threshold(t1, t2)
        tau = functools.reduce(jnp.minimum, top)
        z = functools.reduce(jnp.add, [jnp.exp(t - m) for t in top])
        inv_z = 1.0 / z
        for j in range(nslab):
            rows = slice(j * SUBLANES, (j + 1) * SUBLANES)
            s1j, s2j = s1[rows, :], s2[rows, :]
            cnt = jnp.zeros_like(s1j)
            rank = jnp.zeros_like(s2j)
            for b in range(PEER_TOPK):
                cnt = cnt + jnp.where(s1j + t2[b] >= tau, 1.0, 0.0)
                rank = rank + jnp.where(t2[b] > s2j, 1.0, 0.0)
            c1_ref[hd, rows, :] = cnt
            r2_ref[hd, rows, :] = rank
            e1_ref[hd, rows, :] = jnp.exp(s1j - t1[0])
            e2_ref[hd, rows, :] = jnp.exp(s2j - t2[0]) * inv_z
        return carry

    lax.fori_loop(0, PEER_HEADS, per_head, 0)


def _mid(attn, lru, x2, wo_bf16, g2, wqt_bf16, k1_bf16, k2_bf16, tm):
    T, D = x2.shape
    aw, lw = attn.shape[1], lru.shape[1]
    tab_shape = jax.ShapeDtypeStruct((PEER_HEADS, PEER_N_KEYS, T), _F32)
    tab_spec = pl.BlockSpec((PEER_HEADS, PEER_N_KEYS, tm), lambda i: (0, 0, i))
    const = lambda a: pl.BlockSpec(a.shape, lambda i: (0,) * a.ndim)
    return pl.pallas_call(
        _mid_kernel,
        grid=(T // tm,),
        in_specs=[pl.BlockSpec((tm, aw), lambda i: (i, 0)),
                  pl.BlockSpec((tm, lw), lambda i: (i, 0)),
                  pl.BlockSpec((tm, D), lambda i: (i, 0)),
                  const(wo_bf16), const(g2), const(wqt_bf16), const(k1_bf16), const(k2_bf16)],
        out_specs=[pl.BlockSpec((tm, D), lambda i: (i, 0)),
                   pl.BlockSpec((D, tm), lambda i: (0, i)),
                   tab_spec, tab_spec, tab_spec, tab_spec],
        out_shape=[jax.ShapeDtypeStruct((T, D), _F32),
                   jax.ShapeDtypeStruct((D, T), _BF16),
                   tab_shape, tab_shape, tab_shape, tab_shape],
        scratch_shapes=[pltpu.VMEM((wqt_bf16.shape[0], tm), _F32)],
        compiler_params=_cparams("parallel"),
        name="mid_proj_topk",
    )(attn, lru, x2, wo_bf16, g2, wqt_bf16, k1_bf16, k2_bf16)


def _peer_kernel(xnt_ref, down_ref, upt_ref, c1_ref, e1_ref, r2_ref, e2_ref, h_ref, gf_ref,
                 o_ref, acc_ref, hid_ref, *, te):
    j = pl.program_id(1)

    @pl.when(j == 0)
    def _():
        acc_ref[...] = jnp.zeros(acc_ref.shape, _F32)

    act = jnp.dot(down_ref[...], xnt_ref[...], preferred_element_type=_F32)
    nk = PEER_N_KEYS
    for a in range(te // nk):
        i1 = j * (te // nk) + a
        gate = jnp.zeros((nk, act.shape[1]), _F32)
        for hd in range(PEER_HEADS):
            cnt = c1_ref[hd, pl.ds(i1, 1), :]
            e1 = e1_ref[hd, pl.ds(i1, 1), :]
            gate = gate + jnp.where(r2_ref[hd] < cnt, e2_ref[hd], 0.0) * e1
        hid_ref[a * nk:(a + 1) * nk, :] = (jax.nn.gelu(act[a * nk:(a + 1) * nk, :]) * gate).astype(_BF16)
    acc_ref[...] += jnp.dot(upt_ref[...], hid_ref[...], preferred_element_type=_F32)

    @pl.when(j == pl.num_programs(1) - 1)
    def _():
        o_ref[...] = _rms(h_ref[...] + jnp.transpose(acc_ref[...]), gf_ref[...])


def _peer(xnt, down_bf16, upt_bf16, c1, e1, r2, e2, h, gf, tm, te):
    D, T = xnt.shape
    E = down_bf16.shape[0]
    tab_spec = pl.BlockSpec((PEER_HEADS, PEER_N_KEYS, tm), lambda i, j: (0, 0, i))
    return pl.pallas_call(
        functools.partial(_peer_kernel, te=te),
        grid=(T // tm, E // te),
        in_specs=[pl.BlockSpec((D, tm), lambda i, j: (0, i)),
                  pl.BlockSpec((te, D), lambda i, j: (j, 0)),
                  pl.BlockSpec((D, te), lambda i, j: (0, j)),
                  tab_spec, tab_spec, tab_spec, tab_spec,
                  pl.BlockSpec((tm, D), lambda i, j: (i, 0)),
                  pl.BlockSpec((1, D), lambda i, j: (0, 0))],
        out_specs=pl.BlockSpec((tm, D), lambda i, j: (i, 0)),
        out_shape=jax.ShapeDtypeStruct((T, D), _F32),
        scratch_shapes=[pltpu.VMEM((D, tm), _F32), pltpu.VMEM((te, tm), _BF16)],
        compiler_params=_cparams("parallel", "arbitrary"),
        name="peer_experts",
    )(xnt, down_bf16, upt_bf16, c1, e1, r2, e2, h, gf)


def _block_diag(w):
    n, d, _ = w.shape
    eye = jnp.eye(n, dtype=w.dtype)
    return (eye[:, None, :, None] * w[:, :, None, :]).reshape(n * d, n * d)


def _tile(n, want):
    t = min(n, want)
    assert n % t == 0, (n, want)
    return t


def kernel(x, norm1_g, w_in, lambda_q1, lambda_k1, lambda_q2, lambda_k2, subln_g, conv_w, conv_b,
           w_rec_gate, b_rec_gate, w_in_gate, b_in_gate, lru_lambda, w_out, norm2_g, w_query,
           sub_keys_1, sub_keys_2, expert_down, expert_up, norm_f_g):
    B, S, D = x.shape
    T = B * S
    assert B == 1, "sequence mixing kernels assume a single sequence"
    assert w_in.shape[0] == 1, "the final norm is fused into the (single) layer's PEER kernel"
    l = 0
    h = x.reshape(T, D)
    row = lambda v: v.reshape(1, -1)
    lambda_init = 0.8 - 0.6 * math.exp(-0.3 * l)
    q, k, v, xr, gate = _in_proj(h, row(norm1_g[l]), w_in[l].astype(_BF16), _tile(T, 512))
    attn = _attention(q, k, v, row(lambda_q1[l]), row(lambda_k1[l]), row(lambda_q2[l]),
                      row(lambda_k2[l]), row(subln_g[l]), lambda_init, _tile(T, 512))
    w_gates = jnp.concatenate([_block_diag(w_rec_gate[l]), _block_diag(w_in_gate[l])],
                              axis=1).astype(_BF16)
    lru = _rglru(xr, gate, conv_w[l], row(conv_b[l]), w_gates, row(b_rec_gate[l]),
                 row(b_in_gate[l]), row(lru_lambda[l]), _tile(T, 256))
    h_mid, xnt, c1, e1, r2, e2 = _mid(
        attn, lru, h, w_out[l].astype(_BF16), row(norm2_g[l]),
        jnp.transpose(w_query[l]).astype(_BF16), sub_keys_1[l].astype(_BF16),
        sub_keys_2[l].astype(_BF16), _tile(T, 512))
    out = _peer(xnt, expert_down[l].astype(_BF16), jnp.transpose(expert_up[l]).astype(_BF16),
                c1, e1, r2, e2, h_mid, row(norm_f_g), _tile(T, 512), 512)
    return out.reshape(B, S, D)
```

```python
import functools
import math

import jax
import jax.numpy as jnp
from jax import lax
from jax.experimental import pallas as pl
from jax.experimental.pallas import tpu as pltpu

CHUNK = 64
ATTN_HEAD_DIM = 64
ATTN_HEADS = 4
ATTN_WIDTH = 2 * ATTN_HEAD_DIM * ATTN_HEADS
LRU_BLOCKS = 8
CONV_WIDTH = 4
LRU_C = 8.0
PEER_HEADS = 8
PEER_N_KEYS = 128
PEER_HALF = 128
PEER_TOPK = 16
EPS = 1e-6
NEG_INF = -1e30

LANES = 128
SUBLANES = 8
BF16_VREG_LANES = 256
VMEM_LIMIT_BYTES = 48 * 1024 * 1024

_F32 = jnp.float32
_BF16 = jnp.bfloat16
_NT_DIMS = (((1,), (1,)), ((), ()))


def _cparams(*sem, flags=None):
    return pltpu.CompilerParams(dimension_semantics=sem, vmem_limit_bytes=VMEM_LIMIT_BYTES, flags=flags)


def _rms(x, g):
    return x * lax.rsqrt(jnp.mean(x * x, axis=-1, keepdims=True) + EPS) * g


def _in_proj_kernel(x_ref, g_ref, w_ref, q_ref, k_ref, v_ref, xr_ref, gate_ref):
    n = _rms(x_ref[...], g_ref[...]).astype(_BF16)
    hw = 2 * ATTN_HEAD_DIM

    def proj(c0, width):
        return jnp.dot(n, w_ref[:, c0:c0 + width], preferred_element_type=_F32)

    scale = ATTN_HEAD_DIM ** -0.5 * math.log2(math.e)
    ones_col = (lax.broadcasted_iota(jnp.int32, (x_ref.shape[0], hw), 1) == 0).astype(_BF16)
    for h in range(ATTN_HEADS):
        q_ref[h] = (proj(h * hw, hw) * scale).astype(_BF16)
        k_ref[h] = proj(ATTN_WIDTH + h * hw, hw).astype(_BF16)
        v_ref[h, :, 0:hw] = proj(2 * ATTN_WIDTH + h * hw, hw).astype(_BF16)
        v_ref[h, :, hw:2 * hw] = ones_col
    lru_w = xr_ref.shape[-1]
    xr_ref[...] = proj(3 * ATTN_WIDTH, lru_w)
    gate_ref[...] = proj(3 * ATTN_WIDTH + lru_w, lru_w)


def _in_proj(x2, g, w_bf16, tm):
    T, D = x2.shape
    lru_w = (w_bf16.shape[1] - 3 * ATTN_WIDTH) // 2
    hw = 2 * ATTN_HEAD_DIM
    head_shape = jax.ShapeDtypeStruct((ATTN_HEADS, T, hw), _BF16)
    head_spec = pl.BlockSpec((ATTN_HEADS, tm, hw), lambda i: (0, i, 0))
    v_shape = jax.ShapeDtypeStruct((ATTN_HEADS, T, 2 * hw), _BF16)
    v_spec = pl.BlockSpec((ATTN_HEADS, tm, 2 * hw), lambda i: (0, i, 0))
    row_spec = pl.BlockSpec((tm, lru_w), lambda i: (i, 0))
    return pl.pallas_call(
        _in_proj_kernel,
        grid=(T // tm,),
        in_specs=[pl.BlockSpec((tm, D), lambda i: (i, 0)),
                  pl.BlockSpec((1, D), lambda i: (0, 0)),
                  pl.BlockSpec(w_bf16.shape, lambda i: (0, 0))],
        out_specs=[head_spec, head_spec, v_spec, row_spec, row_spec],
        out_shape=[head_shape, head_shape, v_shape,
                   jax.ShapeDtypeStruct((T, lru_w), _F32), jax.ShapeDtypeStruct((T, lru_w), _F32)],
        compiler_params=_cparams("parallel"),
        name="in_proj",
    )(x2, g, w_bf16)


def _attn_kernel(lq1_ref, lk1_ref, lq2_ref, lk2_ref, sg_ref, q_ref, k_ref, v_ref, o_ref,
                 m_ref, acc_ref, *, blk, nsub, lambda_init):
    i = pl.program_id(1)
    hd = ATTN_HEAD_DIM
    hw = 2 * hd
    lane = lax.broadcasted_iota(jnp.int32, (blk, hw), 1)
    q_maps = []
    for sb in range(nsub):
        q = q_ref[sb * blk:(sb + 1) * blk, :]
        zero = jnp.zeros_like(q)
        q_maps.append((jnp.where(lane < hd, q, zero), jnp.where(lane >= hd, q, zero)))

    m_ref[...] = jnp.full(m_ref.shape, NEG_INF, _F32)
    acc_ref[...] = jnp.zeros(acc_ref.shape, _F32)

    def chain_update(sb, mp, kj, vj, visible):
        s = lax.dot_general(q_maps[sb][mp], kj, _NT_DIMS, preferred_element_type=_F32)
        if visible is not None:
            s = jnp.where(visible, s, NEG_INF)
        idx = 2 * sb + mp
        m_prev = m_ref[idx]
        m_next = jnp.maximum(m_prev, jnp.max(s, axis=1, keepdims=True))
        alpha = jnp.exp2(m_prev - m_next)
        e = jnp.exp2(s - jnp.tile(m_next, (1, blk // LANES)))
        m_ref[idx] = m_next
        pv = jnp.dot(e.astype(_BF16), vj, preferred_element_type=_F32)
        acc_ref[idx] = jnp.tile(alpha, (1, 2 * hw // LANES)) * acc_ref[idx] + pv

    def load_kv(j):
        row0 = pl.multiple_of(j * blk, blk)
        return k_ref[pl.ds(row0, blk), :], v_ref[pl.ds(row0, blk), :]

    def body(j, carry):
        kj, vj = load_kv(j)
        for sb in range(nsub):
            for mp in range(2):
                chain_update(sb, mp, kj, vj, None)
        return carry

    lax.fori_loop(0, nsub * i, body, 0)

    qc = lax.broadcasted_iota(jnp.int32, (blk, blk), 0) // CHUNK
    kc = lax.broadcasted_iota(jnp.int32, (blk, blk), 1) // CHUNK
    diagonal = qc >= kc
    for t in range(nsub):
        kj, vj = load_kv(nsub * i + t)
        for sb in range(t, nsub):
            for mp in range(2):
                chain_update(sb, mp, kj, vj, diagonal if sb == t else None)

    lam = (jnp.exp(jnp.sum(lq1_ref[...] * lk1_ref[...], keepdims=True))
           - jnp.exp(jnp.sum(lq2_ref[...] * lk2_ref[...], keepdims=True)) + lambda_init)
    for sb in range(nsub):
        a0, a1 = acc_ref[2 * sb], acc_ref[2 * sb + 1]
        o = a0[:, 0:hw] / a0[:, hw:hw + 1] - lam * (a1[:, 0:hw] / a1[:, hw:hw + 1])
        o = o * lax.rsqrt(jnp.mean(o * o, axis=-1, keepdims=True) + EPS)
        o_ref[sb * blk:(sb + 1) * blk, :] = (o * sg_ref[...] * (1.0 - lambda_init)).astype(o_ref.dtype)


def _attention(q, k, v, lq1, lk1, lq2, lk2, subln_g, lambda_init, blk, nsub):
    H, T, hw = q.shape
    bq = blk * nsub
    vec = pl.BlockSpec((1, ATTN_HEAD_DIM), lambda h, i: (0, 0))
    kernel = functools.partial(_attn_kernel, blk=blk, nsub=nsub, lambda_init=lambda_init)
    return pl.pallas_call(
        kernel,
        grid=(H, T // bq),
        in_specs=[vec, vec, vec, vec,
                  pl.BlockSpec((1, hw), lambda h, i: (0, 0)),
                  pl.BlockSpec((None, bq, hw), lambda h, i: (h, i, 0)),
                  pl.BlockSpec((None, T, hw), lambda h, i: (h, 0, 0)),
                  pl.BlockSpec((None, T, 2 * hw), lambda h, i: (h, 0, 0))],
        out_specs=pl.BlockSpec((bq, hw), lambda h, i: (i, h)),
        out_shape=jax.ShapeDtypeStruct((T, H * hw), _BF16),
        scratch_shapes=[pltpu.VMEM((2 * nsub, blk, LANES), _F32),
                        pltpu.VMEM((2 * nsub, blk, 2 * hw), _F32)],
        compiler_params=_cparams("parallel", "parallel"),
        name="diff_attention",
    )(lq1, lk1, lq2, lk2, subln_g, q, k, v)


def _shift_rows(x, s, fill):
    rolled = pltpu.roll(x, s, axis=0)
    row = lax.broadcasted_iota(jnp.int32, x.shape, 0)
    return jnp.where(row >= s, rolled, fill)


def _rglru_kernel(xr_ref, gate_ref, cw_ref, cb_ref, wg_ref, ba_ref, bx_ref, lam_ref, o_ref,
                  xbuf_ref, hprev_ref, *, tl):
    halo = SUBLANES
    W = xr_ref.shape[-1]

    @pl.when(pl.program_id(0) == 0)
    def _():
        xbuf_ref[0:halo, :] = jnp.zeros((halo, W), _F32)
        hprev_ref[...] = jnp.zeros(hprev_ref.shape, _F32)

    x = xr_ref[...]
    xbuf_ref[halo:halo + tl, :] = x
    y = cb_ref[...] + x * cw_ref[CONV_WIDTH - 1:CONV_WIDTH, :]
    for t in range(CONV_WIDTH - 1):
        back = CONV_WIDTH - 1 - t
        y = y + xbuf_ref[halo - back:halo - back + tl, :] * cw_ref[t:t + 1, :]
    xbuf_ref[0:halo, :] = x[tl - halo:, :]

    gates = jnp.dot(y.astype(_BF16), wg_ref[...], preferred_element_type=_F32)
    r = jax.nn.sigmoid(gates[:, :W] + ba_ref[...])
    ig = jax.nn.sigmoid(gates[:, W:] + bx_ref[...])
    log_a = -LRU_C * r * jax.nn.softplus(-lam_ref[...])
    a = jnp.exp(log_a)
    one_minus_a2 = -jnp.tanh(log_a) * (1.0 + a * a)
    b = jnp.sqrt(jnp.maximum(one_minus_a2, 1e-12)) * (ig * y)

    s = 1
    while s < tl:
        a_sh = _shift_rows(a, s, 1.0)
        b_sh = _shift_rows(b, s, 0.0)
        b = a * b_sh + b
        a = a * a_sh
        s *= 2
    h = b + a * hprev_ref[0:1, :]
    hprev_ref[...] = jnp.broadcast_to(h[tl - 1:tl, :], hprev_ref.shape)
    o_ref[...] = (h * jax.nn.gelu(gate_ref[...])).astype(o_ref.dtype)


def _rglru(xr, gate, conv_w, conv_b, wg_bf16, b_a, b_x, lru_lambda, tl):
    T, W = xr.shape
    row = pl.BlockSpec((tl, W), lambda i: (i, 0))
    vec = pl.BlockSpec((1, W), lambda i: (0, 0))
    return pl.pallas_call(
        functools.partial(_rglru_kernel, tl=tl),
        grid=(T // tl,),
        in_specs=[row, row,
                  pl.BlockSpec(conv_w.shape, lambda i: (0, 0)),
                  vec,
                  pl.BlockSpec(wg_bf16.shape, lambda i: (0, 0)),
                  vec, vec, vec],
        out_specs=row,
        out_shape=jax.ShapeDtypeStruct((T, W), _BF16),
        scratch_shapes=[pltpu.VMEM((tl + SUBLANES, W), _F32),
                        pltpu.VMEM((SUBLANES, W), _F32)],
        compiler_params=_cparams("arbitrary"),
        name="rglru",
    )(xr, gate, conv_w, conv_b, wg_bf16, b_a, b_x, lru_lambda)


def _oddeven_mergesort_pairs(n):
    pairs = []
    p = 1
    while p < n:
        k = p
        while k >= 1:
            for j in range(k % p, n - k, 2 * k):
                for i in range(min(k, n - j - k)):
                    if (i + j) // (2 * p) == (i + j + k) // (2 * p):
                        pairs.append((i + j, i + j + k))
            k //= 2
        p *= 2
    return pairs


_SORT16 = _oddeven_mergesort_pairs(PEER_TOPK)


def _sort_desc(xs):
    xs = list(xs)
    for lo, hi in _SORT16:
        a, b = xs[lo], xs[hi]
        xs[lo], xs[hi] = jnp.maximum(a, b), jnp.minimum(a, b)
    return xs


def _top_of_union(a_sorted, b_sorted):
    n = len(a_sorted)
    return [jnp.maximum(a_sorted[i], b_sorted[n - 1 - i]) for i in range(n)]


def _bitonic_merge_desc(xs):
    xs = list(xs)
    n = len(xs)
    d = n // 2
    while d >= 1:
        for i in range(n):
            if i & d == 0:
                a, b = xs[i], xs[i + d]
                xs[i], xs[i + d] = jnp.maximum(a, b), jnp.minimum(a, b)
        d //= 2
    return xs


def _prefix_count(pred, vals):
    assert len(vals) == PEER_TOPK == 16
    bits = []
    step = 8
    while step >= 1:
        cands = [vals[base + step - 1] for base in range(0, 16, 2 * step)]
        for b in reversed(bits):
            cands = [jnp.where(b, hi, lo) for lo, hi in zip(cands[0::2], cands[1::2])]
        bits.append(pred(cands[0]))
        step //= 2
    count = jnp.where(pred(vals[15]), 1.0, 0.0)
    for b, w in zip(bits, (8.0, 4.0, 2.0, 1.0)):
        count = count + jnp.where(b, w, 0.0)
    return count


def _top16_rows(s):
    slabs = [s[j * SUBLANES:(j + 1) * SUBLANES, :] for j in range(s.shape[0] // SUBLANES)]
    top = _sort_desc(slabs)
    for shift in (4, 2, 1):
        other = [pltpu.roll(t, shift, axis=0) for t in top]
        top = _bitonic_merge_desc(_top_of_union(top, other))
    return top


def _pair_threshold(t1, t2):
    k = PEER_TOPK
    rows = [[t1[a] + t2[b] for b in range(k // (a + 1))] for a in range(k)]
    neg = jnp.full_like(t1[0], NEG_INF)
    g1 = _sort_desc(rows[1] + rows[2] + rows[4])
    g2 = _sort_desc(rows[3] + rows[5] + rows[6] + rows[7]
                    + [rows[a][0] for a in range(8, 14)])
    g3 = _sort_desc([rows[14][0], rows[15][0]] + [neg] * (k - 2))
    top = _bitonic_merge_desc(_top_of_union(rows[0], g1))
    top = _bitonic_merge_desc(_top_of_union(top, g2))
    top = _top_of_union(top, g3)
    return top, rows[0][0]


def _mid_kernel(attn_ref, lru_ref, x_ref, wo_ref, g2_ref, wqt_ref, k1_ref, k2_ref,
                h_ref, xn_ref, c1_ref, e1_ref, r2_ref, e2_ref, qt_ref):
    aw = attn_ref.shape[-1]
    h = (x_ref[...]
         + jnp.dot(attn_ref[...], wo_ref[0:aw, :], preferred_element_type=_F32)
         + jnp.dot(lru_ref[...], wo_ref[aw:, :], preferred_element_type=_F32))
    h_ref[...] = h
    xn = _rms(h, g2_ref[...]).astype(_BF16)
    xn_ref[...] = xn
    qt_ref[...] = lax.dot_general(wqt_ref[...], xn, _NT_DIMS, preferred_element_type=_F32)

    nslab = PEER_N_KEYS // SUBLANES

    def per_head(hd, carry):
        base = pl.multiple_of(hd * (2 * PEER_HALF), 2 * PEER_HALF)
        q1 = qt_ref[pl.ds(base, PEER_HALF), :].astype(_BF16)
        q2 = qt_ref[pl.ds(base + PEER_HALF, PEER_HALF), :].astype(_BF16)
        s1 = jnp.dot(k1_ref[...], q1, preferred_element_type=_F32)
        s2 = jnp.dot(k2_ref[...], q2, preferred_element_type=_F32)
        t1 = _top16_rows(s1)
        t2 = _top16_rows(s2)
        top, m = _pair_threshold(t1, t2)
        tau = functools.reduce(jnp.minimum, top)
        z = functools.reduce(jnp.add, [jnp.exp(t - m) for t in top])
        inv_z = 1.0 / z
        ranks, e2s = [], []
        for j in range(nslab):
            rows = slice(j * SUBLANES, (j + 1) * SUBLANES)
            s1j, s2j = s1[rows, :], s2[rows, :]
            cnt = _prefix_count(lambda v: s1j + v >= tau, t2)
            rank = _prefix_count(lambda v: v > s2j, t2)
            c1_ref[hd, rows, :] = cnt
            e1_ref[hd, rows, :] = jnp.exp(s1j - t1[0])
            ranks.append(rank)
            e2s.append(jnp.exp(s2j - t2[0]) * inv_z)
        r2_ref[hd] = jnp.concatenate(ranks, axis=0).astype(_BF16)
        e2_ref[hd] = jnp.concatenate(e2s, axis=0).astype(_BF16)
        return carry

    lax.fori_loop(0, PEER_HEADS, per_head, 0)


def _mid(attn, lru, x2, wo_bf16, g2, wqt_bf16, k1_bf16, k2_bf16, tm):
    T, D = x2.shape
    aw, lw = attn.shape[1], lru.shape[1]
    tab_f32 = jax.ShapeDtypeStruct((PEER_HEADS, PEER_N_KEYS, T), _F32)
    tab_bf16 = jax.ShapeDtypeStruct((PEER_HEADS, PEER_N_KEYS, T), _BF16)
    tab_spec = pl.BlockSpec((PEER_HEADS, PEER_N_KEYS, tm), lambda i: (0, 0, i))
    const = lambda a: pl.BlockSpec(a.shape, lambda i: (0,) * a.ndim)
    return pl.pallas_call(
        _mid_kernel,
        grid=(T // tm,),
        in_specs=[pl.BlockSpec((tm, aw), lambda i: (i, 0)),
                  pl.BlockSpec((tm, lw), lambda i: (i, 0)),
                  pl.BlockSpec((tm, D), lambda i: (i, 0)),
                  const(wo_bf16), const(g2), const(wqt_bf16), const(k1_bf16), const(k2_bf16)],
        out_specs=[pl.BlockSpec((tm, D), lambda i: (i, 0)),
                   pl.BlockSpec((tm, D), lambda i: (i, 0)),
                   tab_spec, tab_spec, tab_spec, tab_spec],
        out_shape=[jax.ShapeDtypeStruct((T, D), _F32),
                   jax.ShapeDtypeStruct((T, D), _BF16),
                   tab_f32, tab_f32, tab_bf16, tab_bf16],
        scratch_shapes=[pltpu.VMEM((wqt_bf16.shape[0], tm), _F32)],
        compiler_params=_cparams("parallel"),
        name="mid_proj_topk",
    )(attn, lru, x2, wo_bf16, g2, wqt_bf16, k1_bf16, k2_bf16)


def _gelu_tanh(x):
    c1 = 2.0 * math.sqrt(2.0 / math.pi)
    c2 = c1 * 0.044715
    return x / (1.0 + jnp.exp(x * (-c1 - c2 * (x * x))))


def _peer_kernel(xn_ref, down_ref, upt_ref, c1_ref, e1_ref, r2_ref, e2_ref, h_ref, gf_ref,
                 o_ref, acc_ref, hid_ref, *, te):
    j = pl.program_id(1)
    nk = PEER_N_KEYS
    tm = xn_ref.shape[0]
    W = BF16_VREG_LANES
    R = 64
    nb = R // SUBLANES
    mb = 2 * nk

    @pl.when(j == 0)
    def _():
        acc_ref[...] = jnp.zeros(acc_ref.shape, _F32)

    row0 = pl.multiple_of(j * (te // nk), SUBLANES)
    zero = jnp.zeros((nb, SUBLANES, W), _BF16)
    xn = xn_ref[...]
    for m in range(te // mb):
        act = lax.dot_general(down_ref[m * mb:(m + 1) * mb, :], xn, _NT_DIMS,
                              preferred_element_type=_F32)
        for c in range(tm // W):
            lanes = slice(c * W, (c + 1) * W)
            for a in range(mb // nk):
                i1 = m * (mb // nk) + a
                cnts, e1s = [], []
                for hd in range(PEER_HEADS):
                    c1_row = c1_ref[hd, pl.ds(row0, SUBLANES), lanes][i1:i1 + 1, :]
                    e1_row = e1_ref[hd, pl.ds(row0, SUBLANES), lanes][i1:i1 + 1, :]
                    cnts.append(jnp.broadcast_to(c1_row, (SUBLANES, W)).astype(_BF16)[None])
                    e1s.append(jnp.broadcast_to(e1_row, (SUBLANES, W)).astype(_BF16)[None])
                for rb in range(nk // R):
                    gate = zero
                    for hd in range(PEER_HEADS):
                        r2 = r2_ref[hd, rb * R:(rb + 1) * R, lanes].reshape(nb, SUBLANES, W)
                        e2 = e2_ref[hd, rb * R:(rb + 1) * R, lanes].reshape(nb, SUBLANES, W)
                        gate = gate + jnp.where(r2 < cnts[hd], e2, zero) * e1s[hd]
                    g = _gelu_tanh(act[a * nk + rb * R:a * nk + (rb + 1) * R, lanes]).astype(_BF16)
                    rows = slice(i1 * nk + rb * R, i1 * nk + (rb + 1) * R)
                    hid_ref[rows, lanes] = (g.reshape(nb, SUBLANES, W) * gate).reshape(R, W)
    acc_ref[...] += jnp.dot(upt_ref[...], hid_ref[...], preferred_element_type=_F32)

    @pl.when(j == pl.num_programs(1) - 1)
    def _():
        o_ref[...] = _rms(h_ref[...] + jnp.transpose(acc_ref[...]), gf_ref[...])


def _peer(xn, down_bf16, upt_bf16, c1, e1, r2, e2, h, gf, tm, te):
    T, D = xn.shape
    E = down_bf16.shape[0]
    nk = PEER_N_KEYS
    assert te == nk * SUBLANES, "an expert tile spans one sublane group of first-key rows"
    tab_spec = pl.BlockSpec((PEER_HEADS, nk, tm), lambda i, j: (0, 0, i))
    return pl.pallas_call(
        functools.partial(_peer_kernel, te=te),
        grid=(T // tm, E // te),
        in_specs=[pl.BlockSpec((tm, D), lambda i, j: (i, 0)),
                  pl.BlockSpec((te, D), lambda i, j: (j, 0)),
                  pl.BlockSpec((D, te), lambda i, j: (0, j)),
                  tab_spec, tab_spec, tab_spec, tab_spec,
                  pl.BlockSpec((tm, D), lambda i, j: (i, 0)),
                  pl.BlockSpec((1, D), lambda i, j: (0, 0))],
        out_specs=pl.BlockSpec((tm, D), lambda i, j: (i, 0)),
        out_shape=jax.ShapeDtypeStruct((T, D), _F32),
        scratch_shapes=[pltpu.VMEM((D, tm), _F32), pltpu.VMEM((te, tm), _BF16)],
        compiler_params=_cparams("parallel", "arbitrary"),
        name="peer_experts",
    )(xn, down_bf16, upt_bf16, c1, e1, r2, e2, h, gf)


def _block_diag(w):
    n, d, _ = w.shape
    eye = jnp.eye(n, dtype=w.dtype)
    return (eye[:, None, :, None] * w[:, :, None, :]).reshape(n * d, n * d)


def _tile(n, want):
    t = min(n, want)
    assert n % t == 0, (n, want)
    return t


def kernel(x, norm1_g, w_in, lambda_q1, lambda_k1, lambda_q2, lambda_k2, subln_g, conv_w, conv_b,
           w_rec_gate, b_rec_gate, w_in_gate, b_in_gate, lru_lambda, w_out, norm2_g, w_query,
           sub_keys_1, sub_keys_2, expert_down, expert_up, norm_f_g):
    B, S, D = x.shape
    T = B * S
    assert B == 1, "sequence mixing kernels assume a single sequence"
    assert w_in.shape[0] == 1, "the final norm is fused into the (single) layer's PEER kernel"
    l = 0
    h = x.reshape(T, D)
    row = lambda v: v.reshape(1, -1)
    lambda_init = 0.8 - 0.6 * math.exp(-0.3 * l)
    q, k, v, xr, gate = _in_proj(h, row(norm1_g[l]), w_in[l].astype(_BF16), _tile(T, 512))
    attn = _attention(q, k, v, row(lambda_q1[l]), row(lambda_k1[l]), row(lambda_q2[l]),
                      row(lambda_k2[l]), row(subln_g[l]), lambda_init, _tile(T, 512), 2)
    w_gates = jnp.concatenate([_block_diag(w_rec_gate[l]), _block_diag(w_in_gate[l])],
                              axis=1).astype(_BF16)
    lru = _rglru(xr, gate, conv_w[l], row(conv_b[l]), w_gates, row(b_rec_gate[l]),
                 row(b_in_gate[l]), row(lru_lambda[l]), _tile(T, 256))
    h_mid, xnt, c1, e1, r2, e2 = _mid(
        attn, lru, h, w_out[l].astype(_BF16), row(norm2_g[l]),
        jnp.transpose(w_query[l]).astype(_BF16), sub_keys_1[l].astype(_BF16),
        sub_keys_2[l].astype(_BF16), _tile(T, 512))
    out = _peer(xnt, expert_down[l].astype(_BF16), jnp.transpose(expert_up[l]).astype(_BF16),
                c1, e1, r2, e2, h_mid, row(norm_f_g), _tile(T, 512), 1024)
    return out.reshape(B, S, D)
```

```python
import functools
import math

import jax
import jax.numpy as jnp
from jax import lax
from jax.experimental import pallas as pl
from jax.experimental.pallas import tpu as pltpu

CHUNK = 64
ATTN_HEAD_DIM = 64
ATTN_HEADS = 4
ATTN_WIDTH = 2 * ATTN_HEAD_DIM * ATTN_HEADS
LRU_BLOCKS = 8
CONV_WIDTH = 4
LRU_C = 8.0
PEER_HEADS = 8
PEER_N_KEYS = 128
PEER_HALF = 128
PEER_TOPK = 16
EPS = 1e-6
NEG_INF = -1e30

LANES = 128
SUBLANES = 8
BF16_VREG_LANES = 256
VMEM_LIMIT_BYTES = 48 * 1024 * 1024

_F32 = jnp.float32
_BF16 = jnp.bfloat16
_NT_DIMS = (((1,), (1,)), ((), ()))


def _cparams(*sem, flags=None):
    return pltpu.CompilerParams(dimension_semantics=sem, vmem_limit_bytes=VMEM_LIMIT_BYTES, flags=flags)


def _rms(x, g):
    return x * lax.rsqrt(jnp.mean(x * x, axis=-1, keepdims=True) + EPS) * g


def _in_proj_kernel(x_ref, g_ref, w_ref, q_ref, k_ref, v_ref, xr_ref, gate_ref):
    n = _rms(x_ref[...], g_ref[...]).astype(_BF16)
    hw = 2 * ATTN_HEAD_DIM

    def proj(c0, width):
        return jnp.dot(n, w_ref[:, c0:c0 + width], preferred_element_type=_F32)

    scale = ATTN_HEAD_DIM ** -0.5 * math.log2(math.e)
    ones_col = (lax.broadcasted_iota(jnp.int32, (x_ref.shape[0], hw), 1) == 0).astype(_BF16)
    for h in range(ATTN_HEADS):
        q_ref[h] = (proj(h * hw, hw) * scale).astype(_BF16)
        k_ref[h] = proj(ATTN_WIDTH + h * hw, hw).astype(_BF16)
        v_ref[h, :, 0:hw] = proj(2 * ATTN_WIDTH + h * hw, hw).astype(_BF16)
        v_ref[h, :, hw:2 * hw] = ones_col
    lru_w = xr_ref.shape[-1]
    xr_ref[...] = proj(3 * ATTN_WIDTH, lru_w)
    gate_ref[...] = proj(3 * ATTN_WIDTH + lru_w, lru_w)


def _in_proj(x2, g, w_bf16, tm):
    T, D = x2.shape
    lru_w = (w_bf16.shape[1] - 3 * ATTN_WIDTH) // 2
    hw = 2 * ATTN_HEAD_DIM
    head_shape = jax.ShapeDtypeStruct((ATTN_HEADS, T, hw), _BF16)
    head_spec = pl.BlockSpec((ATTN_HEADS, tm, hw), lambda i: (0, i, 0))
    v_shape = jax.ShapeDtypeStruct((ATTN_HEADS, T, 2 * hw), _BF16)
    v_spec = pl.BlockSpec((ATTN_HEADS, tm, 2 * hw), lambda i: (0, i, 0))
    row_spec = pl.BlockSpec((tm, lru_w), lambda i: (i, 0))
    return pl.pallas_call(
        _in_proj_kernel,
        grid=(T // tm,),
        in_specs=[pl.BlockSpec((tm, D), lambda i: (i, 0)),
                  pl.BlockSpec((1, D), lambda i: (0, 0)),
                  pl.BlockSpec(w_bf16.shape, lambda i: (0, 0))],
        out_specs=[head_spec, head_spec, v_spec, row_spec, row_spec],
        out_shape=[head_shape, head_shape, v_shape,
                   jax.ShapeDtypeStruct((T, lru_w), _F32), jax.ShapeDtypeStruct((T, lru_w), _F32)],
        compiler_params=_cparams("parallel"),
        name="in_proj",
    )(x2, g, w_bf16)


def _attn_kernel(lq1_ref, lk1_ref, lq2_ref, lk2_ref, sg_ref, q_ref, k_ref, v_ref, o_ref,
                 m_ref, acc_ref, *, blk, nsub, lambda_init):
    i = pl.program_id(1)
    hd = ATTN_HEAD_DIM
    hw = 2 * hd
    lane = lax.broadcasted_iota(jnp.int32, (blk, hw), 1)
    q_maps = []
    for sb in range(nsub):
        q = q_ref[sb * blk:(sb + 1) * blk, :]
        zero = jnp.zeros_like(q)
        q_maps.append((jnp.where(lane < hd, q, zero), jnp.where(lane >= hd, q, zero)))

    m_ref[...] = jnp.full(m_ref.shape, NEG_INF, _F32)
    acc_ref[...] = jnp.zeros(acc_ref.shape, _F32)

    def chain_update(sb, mp, kj, vj, visible):
        s = lax.dot_general(q_maps[sb][mp], kj, _NT_DIMS, preferred_element_type=_F32)
        if visible is not None:
            s = jnp.where(visible, s, NEG_INF)
        idx = 2 * sb + mp
        m_prev = m_ref[idx]
        m_next = jnp.maximum(m_prev, jnp.max(s, axis=1, keepdims=True))
        alpha = jnp.exp2(m_prev - m_next)
        e = jnp.exp2(s - jnp.tile(m_next, (1, blk // LANES)))
        m_ref[idx] = m_next
        pv = jnp.dot(e.astype(_BF16), vj, preferred_element_type=_F32)
        acc_ref[idx] = jnp.tile(alpha, (1, 2 * hw // LANES)) * acc_ref[idx] + pv

    def load_kv(j):
        row0 = pl.multiple_of(j * blk, blk)
        return k_ref[pl.ds(row0, blk), :], v_ref[pl.ds(row0, blk), :]

    def body(j, carry):
        kj, vj = load_kv(j)
        for sb in range(nsub):
            for mp in range(2):
                chain_update(sb, mp, kj, vj, None)
        return carry

    lax.fori_loop(0, nsub * i, body, 0)

    qc = lax.broadcasted_iota(jnp.int32, (blk, blk), 0) // CHUNK
    kc = lax.broadcasted_iota(jnp.int32, (blk, blk), 1) // CHUNK
    diagonal = qc >= kc
    for t in range(nsub):
        kj, vj = load_kv(nsub * i + t)
        for sb in range(t, nsub):
            for mp in range(2):
                chain_update(sb, mp, kj, vj, diagonal if sb == t else None)

    lam = (jnp.exp(jnp.sum(lq1_ref[...] * lk1_ref[...], keepdims=True))
           - jnp.exp(jnp.sum(lq2_ref[...] * lk2_ref[...], keepdims=True)) + lambda_init)
    for sb in range(nsub):
        a0, a1 = acc_ref[2 * sb], acc_ref[2 * sb + 1]
        o = a0[:, 0:hw] / a0[:, hw:hw + 1] - lam * (a1[:, 0:hw] / a1[:, hw:hw + 1])
        o = o * lax.rsqrt(jnp.mean(o * o, axis=-1, keepdims=True) + EPS)
        o_ref[sb * blk:(sb + 1) * blk, :] = (o * sg_ref[...] * (1.0 - lambda_init)).astype(o_ref.dtype)


def _attention(q, k, v, lq1, lk1, lq2, lk2, subln_g, lambda_init, blk, nsub):
    H, T, hw = q.shape
    bq = blk * nsub
    vec = pl.BlockSpec((1, ATTN_HEAD_DIM), lambda h, i: (0, 0))
    kernel = functools.partial(_attn_kernel, blk=blk, nsub=nsub, lambda_init=lambda_init)
    return pl.pallas_call(
        kernel,
        grid=(H, T // bq),
        in_specs=[vec, vec, vec, vec,
                  pl.BlockSpec((1, hw), lambda h, i: (0, 0)),
                  pl.BlockSpec((None, bq, hw), lambda h, i: (h, i, 0)),
                  pl.BlockSpec((None, T, hw), lambda h, i: (h, 0, 0)),
                  pl.BlockSpec((None, T, 2 * hw), lambda h, i: (h, 0, 0))],
        out_specs=pl.BlockSpec((bq, hw), lambda h, i: (i, h)),
        out_shape=jax.ShapeDtypeStruct((T, H * hw), _BF16),
        scratch_shapes=[pltpu.VMEM((2 * nsub, blk, LANES), _F32),
                        pltpu.VMEM((2 * nsub, blk, 2 * hw), _F32)],
        compiler_params=_cparams("parallel", "parallel"),
        name="diff_attention",
    )(lq1, lk1, lq2, lk2, subln_g, q, k, v)


def _shift_rows(x, s, fill):
    rolled = pltpu.roll(x, s, axis=0)
    row = lax.broadcasted_iota(jnp.int32, x.shape, 0)
    return jnp.where(row >= s, rolled, fill)


def _rglru_kernel(xr_ref, gate_ref, cw_ref, cb_ref, wg_ref, ba_ref, bx_ref, lam_ref, o_ref,
                  xbuf_ref, hprev_ref, *, tl):
    halo = SUBLANES
    W = xr_ref.shape[-1]

    @pl.when(pl.program_id(0) == 0)
    def _():
        xbuf_ref[0:halo, :] = jnp.zeros((halo, W), _F32)
        hprev_ref[...] = jnp.zeros(hprev_ref.shape, _F32)

    x = xr_ref[...]
    xbuf_ref[halo:halo + tl, :] = x
    y = cb_ref[...] + x * cw_ref[CONV_WIDTH - 1:CONV_WIDTH, :]
    for t in range(CONV_WIDTH - 1):
        back = CONV_WIDTH - 1 - t
        y = y + xbuf_ref[halo - back:halo - back + tl, :] * cw_ref[t:t + 1, :]
    xbuf_ref[0:halo, :] = x[tl - halo:, :]

    gates = jnp.dot(y.astype(_BF16), wg_ref[...], preferred_element_type=_F32)
    r = jax.nn.sigmoid(gates[:, :W] + ba_ref[...])
    ig = jax.nn.sigmoid(gates[:, W:] + bx_ref[...])
    log_a = -LRU_C * r * jax.nn.softplus(-lam_ref[...])
    a = jnp.exp(log_a)
    one_minus_a2 = -jnp.tanh(log_a) * (1.0 + a * a)
    b = jnp.sqrt(jnp.maximum(one_minus_a2, 1e-12)) * (ig * y)

    s = 1
    while s < tl:
        a_sh = _shift_rows(a, s, 1.0)
        b_sh = _shift_rows(b, s, 0.0)
        b = a * b_sh + b
        a = a * a_sh
        s *= 2
    h = b + a * hprev_ref[0:1, :]
    hprev_ref[...] = jnp.broadcast_to(h[tl - 1:tl, :], hprev_ref.shape)
    o_ref[...] = (h * jax.nn.gelu(gate_ref[...])).astype(o_ref.dtype)


def _rglru(xr, gate, conv_w, conv_b, wg_bf16, b_a, b_x, lru_lambda, tl):
    T, W = xr.shape
    row = pl.BlockSpec((tl, W), lambda i: (i, 0))
    vec = pl.BlockSpec((1, W), lambda i: (0, 0))
    return pl.pallas_call(
        functools.partial(_rglru_kernel, tl=tl),
        grid=(T // tl,),
        in_specs=[row, row,
                  pl.BlockSpec(conv_w.shape, lambda i: (0, 0)),
                  vec,
                  pl.BlockSpec(wg_bf16.shape, lambda i: (0, 0)),
                  vec, vec, vec],
        out_specs=row,
        out_shape=jax.ShapeDtypeStruct((T, W), _BF16),
        scratch_shapes=[pltpu.VMEM((tl + SUBLANES, W), _F32),
                        pltpu.VMEM((SUBLANES, W), _F32)],
        compiler_params=_cparams("arbitrary"),
        name="rglru",
    )(xr, gate, conv_w, conv_b, wg_bf16, b_a, b_x, lru_lambda)


def _oddeven_mergesort_pairs(n):
    pairs = []
    p = 1
    while p < n:
        k = p
        while k >= 1:
            for j in range(k % p, n - k, 2 * k):
                for i in range(min(k, n - j - k)):
                    if (i + j) // (2 * p) == (i + j + k) // (2 * p):
                        pairs.append((i + j, i + j + k))
            k //= 2
        p *= 2
    return pairs


_SORT16 = _oddeven_mergesort_pairs(PEER_TOPK)


def _sort_desc(xs):
    xs = list(xs)
    for lo, hi in _SORT16:
        a, b = xs[lo], xs[hi]
        xs[lo], xs[hi] = jnp.maximum(a, b), jnp.minimum(a, b)
    return xs


def _top_of_union(a_sorted, b_sorted):
    n = len(a_sorted)
    return [jnp.maximum(a_sorted[i], b_sorted[n - 1 - i]) for i in range(n)]


def _bitonic_merge_desc(xs):
    xs = list(xs)
    n = len(xs)
    d = n // 2
    while d >= 1:
        for i in range(n):
            if i & d == 0:
                a, b = xs[i], xs[i + d]
                xs[i], xs[i + d] = jnp.maximum(a, b), jnp.minimum(a, b)
        d //= 2
    return xs


def _prefix_count(pred, vals):
    assert len(vals) == PEER_TOPK == 16
    bits = []
    step = 8
    while step >= 1:
        cands = [vals[base + step - 1] for base in range(0, 16, 2 * step)]
        for b in reversed(bits):
            cands = [jnp.where(b, hi, lo) for lo, hi in zip(cands[0::2], cands[1::2])]
        bits.append(pred(cands[0]))
        step //= 2
    count = jnp.where(pred(vals[15]), 1.0, 0.0)
    for b, w in zip(bits, (8.0, 4.0, 2.0, 1.0)):
        count = count + jnp.where(b, w, 0.0)
    return count


def _pair_threshold(t1, t2):
    k = PEER_TOPK
    rows = [[t1[a] + t2[b] for b in range(k // (a + 1))] for a in range(k)]
    neg = jnp.full_like(t1[0], NEG_INF)
    g1 = _sort_desc(rows[1] + rows[2] + rows[4])
    g2 = _sort_desc(rows[3] + rows[5] + rows[6] + rows[7]
                    + [rows[a][0] for a in range(8, 14)])
    g3 = _sort_desc([rows[14][0], rows[15][0]] + [neg] * (k - 2))
    top = _bitonic_merge_desc(_top_of_union(rows[0], g1))
    top = _bitonic_merge_desc(_top_of_union(top, g2))
    top = _top_of_union(top, g3)
    return top, rows[0][0]


def _mid_kernel(attn_ref, lru_ref, x_ref, wo_ref, g2_ref, wqt_ref, k1_ref, k2_ref,
                h_ref, xn_ref, c1_ref, e1_ref, r2_ref, e2_ref, qt_ref, sorted_ref, top2_ref, stat_ref):
    aw = attn_ref.shape[-1]
    h = (x_ref[...]
         + jnp.dot(attn_ref[...], wo_ref[0:aw, :], preferred_element_type=_F32)
         + jnp.dot(lru_ref[...], wo_ref[aw:, :], preferred_element_type=_F32))
    h_ref[...] = h
    xn = _rms(h, g2_ref[...]).astype(_BF16)
    xn_ref[...] = xn
    qt_ref[...] = lax.dot_general(wqt_ref[...], xn, _NT_DIMS, preferred_element_type=_F32)

    nslab = PEER_N_KEYS // SUBLANES
    tm = qt_ref.shape[-1]
    assert PEER_HEADS == SUBLANES, "phase 2 puts one head on each sublane"

    def scores(hd):
        base = pl.multiple_of(hd * (2 * PEER_HALF), 2 * PEER_HALF)
        q1 = qt_ref[pl.ds(base, PEER_HALF), :].astype(_BF16)
        q2 = qt_ref[pl.ds(base + PEER_HALF, PEER_HALF), :].astype(_BF16)
        return (jnp.dot(k1_ref[...], q1, preferred_element_type=_F32),
                jnp.dot(k2_ref[...], q2, preferred_element_type=_F32))

    def sort_head(hd, carry):
        for side, s in enumerate(scores(hd)):
            srt = _sort_desc([s[j * SUBLANES:(j + 1) * SUBLANES, :] for j in range(nslab)])
            for b in range(PEER_TOPK):
                for sl in range(SUBLANES):
                    sorted_ref[side, b, sl, pl.ds(hd, 1), :] = srt[b][sl:sl + 1, :]
        return carry

    lax.fori_loop(0, PEER_HEADS, sort_head, 0)

    def merged_top16(side):
        lists = [[sorted_ref[side, b, sl] for b in range(PEER_TOPK)] for sl in range(SUBLANES)]
        while len(lists) > 1:
            lists = [_bitonic_merge_desc(_top_of_union(a, b)) for a, b in zip(lists[0::2], lists[1::2])]
        return lists[0]

    t1, t2 = merged_top16(0), merged_top16(1)
    top, m = _pair_threshold(t1, t2)
    z = functools.reduce(jnp.add, [jnp.exp(t - m) for t in top])
    for b in range(PEER_TOPK):
        top2_ref[b] = t2[b]
    stat_ref[0] = functools.reduce(jnp.minimum, top)
    stat_ref[1] = t1[0]
    stat_ref[2] = 1.0 / z

    def tables(hd, carry):
        def head_row(ref, k):
            return jnp.broadcast_to(ref[k, pl.ds(hd, 1), :], (SUBLANES, tm))

        s1, s2 = scores(hd)
        t2h = [head_row(top2_ref, b) for b in range(PEER_TOPK)]
        tau, m1, inv_z = head_row(stat_ref, 0), head_row(stat_ref, 1), head_row(stat_ref, 2)
        ranks, e2s = [], []
        for j in range(nslab):
            rows = slice(j * SUBLANES, (j + 1) * SUBLANES)
            s1j, s2j = s1[rows, :], s2[rows, :]
            cnt = _prefix_count(lambda v: s1j + v >= tau, t2h)
            rank = _prefix_count(lambda v: v > s2j, t2h)
            c1_ref[hd, rows, :] = cnt
            e1_ref[hd, rows, :] = jnp.exp(s1j - m1)
            ranks.append(rank)
            e2s.append(jnp.exp(s2j - t2h[0]) * inv_z)
        r2_ref[hd] = jnp.concatenate(ranks, axis=0).astype(_BF16)
        e2_ref[hd] = jnp.concatenate(e2s, axis=0).astype(_BF16)
        return carry

    lax.fori_loop(0, PEER_HEADS, tables, 0)


def _mid(attn, lru, x2, wo_bf16, g2, wqt_bf16, k1_bf16, k2_bf16, tm):
    T, D = x2.shape
    aw, lw = attn.shape[1], lru.shape[1]
    tab_f32 = jax.ShapeDtypeStruct((PEER_HEADS, PEER_N_KEYS, T), _F32)
    tab_bf16 = jax.ShapeDtypeStruct((PEER_HEADS, PEER_N_KEYS, T), _BF16)
    tab_spec = pl.BlockSpec((PEER_HEADS, PEER_N_KEYS, tm), lambda i: (0, 0, i))
    const = lambda a: pl.BlockSpec(a.shape, lambda i: (0,) * a.ndim)
    return pl.pallas_call(
        _mid_kernel,
        grid=(T // tm,),
        in_specs=[pl.BlockSpec((tm, aw), lambda i: (i, 0)),
                  pl.BlockSpec((tm, lw), lambda i: (i, 0)),
                  pl.BlockSpec((tm, D), lambda i: (i, 0)),
                  const(wo_bf16), const(g2), const(wqt_bf16), const(k1_bf16), const(k2_bf16)],
        out_specs=[pl.BlockSpec((tm, D), lambda i: (i, 0)),
                   pl.BlockSpec((tm, D), lambda i: (i, 0)),
                   tab_spec, tab_spec, tab_spec, tab_spec],
        out_shape=[jax.ShapeDtypeStruct((T, D), _F32),
                   jax.ShapeDtypeStruct((T, D), _BF16),
                   tab_f32, tab_f32, tab_bf16, tab_bf16],
        scratch_shapes=[pltpu.VMEM((wqt_bf16.shape[0], tm), _F32),
                        pltpu.VMEM((2, PEER_TOPK, SUBLANES, PEER_HEADS, tm), _F32),
                        pltpu.VMEM((PEER_TOPK, PEER_HEADS, tm), _F32),
                        pltpu.VMEM((3, PEER_HEADS, tm), _F32)],
        compiler_params=_cparams("parallel"),
        name="mid_proj_topk",
    )(attn, lru, x2, wo_bf16, g2, wqt_bf16, k1_bf16, k2_bf16)


def _gelu_tanh(x):
    c1 = 2.0 * math.sqrt(2.0 / math.pi)
    c2 = c1 * 0.044715
    return x / (1.0 + jnp.exp(x * (-c1 - c2 * (x * x))))


def _peer_kernel(xn_ref, down_ref, upt_ref, c1_ref, e1_ref, r2_ref, e2_ref, h_ref, gf_ref,
                 o_ref, acc_ref, hid_ref, *, te):
    j = pl.program_id(1)
    nk = PEER_N_KEYS
    tm = xn_ref.shape[0]
    W = BF16_VREG_LANES
    R = 64
    nb = R // SUBLANES
    mb = 2 * nk

    @pl.when(j == 0)
    def _():
        acc_ref[...] = jnp.zeros(acc_ref.shape, _F32)

    row0 = pl.multiple_of(j * (te // nk), SUBLANES)
    zero = jnp.zeros((nb, SUBLANES, W), _BF16)
    xn = xn_ref[...]
    for m in range(te // mb):
        act = lax.dot_general(down_ref[m * mb:(m + 1) * mb, :], xn, _NT_DIMS,
                              preferred_element_type=_F32)
        for c in range(tm // W):
            lanes = slice(c * W, (c + 1) * W)
            for a in range(mb // nk):
                i1 = m * (mb // nk) + a
                cnts, e1s = [], []
                for hd in range(PEER_HEADS):
                    c1_row = c1_ref[hd, pl.ds(row0, SUBLANES), lanes][i1:i1 + 1, :]
                    e1_row = e1_ref[hd, pl.ds(row0, SUBLANES), lanes][i1:i1 + 1, :]
                    cnts.append(jnp.broadcast_to(c1_row, (SUBLANES, W)).astype(_BF16)[None])
                    e1s.append(jnp.broadcast_to(e1_row, (SUBLANES, W)).astype(_BF16)[None])
                for rb in range(nk // R):
                    gate = zero
                    for hd in range(PEER_HEADS):
                        r2 = r2_ref[hd, rb * R:(rb + 1) * R, lanes].reshape(nb, SUBLANES, W)
                        e2 = e2_ref[hd, rb * R:(rb + 1) * R, lanes].reshape(nb, SUBLANES, W)
                        gate = gate + jnp.where(r2 < cnts[hd], e2, zero) * e1s[hd]
                    g = _gelu_tanh(act[a * nk + rb * R:a * nk + (rb + 1) * R, lanes].astype(_BF16))
                    rows = slice(i1 * nk + rb * R, i1 * nk + (rb + 1) * R)
                    hid_ref[rows, lanes] = (g.reshape(nb, SUBLANES, W) * gate).reshape(R, W)
    acc_ref[...] += jnp.dot(upt_ref[...], hid_ref[...], preferred_element_type=_F32)

    @pl.when(j == pl.num_programs(1) - 1)
    def _():
        o_ref[...] = _rms(h_ref[...] + jnp.transpose(acc_ref[...]), gf_ref[...])


def _peer(xn, down_bf16, upt_bf16, c1, e1, r2, e2, h, gf, tm, te):
    T, D = xn.shape
    E = down_bf16.shape[0]
    nk = PEER_N_KEYS
    assert te == nk * SUBLANES, "an expert tile spans one sublane group of first-key rows"
    tab_spec = pl.BlockSpec((PEER_HEADS, nk, tm), lambda i, j: (0, 0, i))
    return pl.pallas_call(
        functools.partial(_peer_kernel, te=te),
        grid=(T // tm, E // te),
        in_specs=[pl.BlockSpec((tm, D), lambda i, j: (i, 0)),
                  pl.BlockSpec((te, D), lambda i, j: (j, 0)),
                  pl.BlockSpec((D, te), lambda i, j: (0, j)),
                  tab_spec, tab_spec, tab_spec, tab_spec,
                  pl.BlockSpec((tm, D), lambda i, j: (i, 0)),
                  pl.BlockSpec((1, D), lambda i, j: (0, 0))],
        out_specs=pl.BlockSpec((tm, D), lambda i, j: (i, 0)),
        out_shape=jax.ShapeDtypeStruct((T, D), _F32),
        scratch_shapes=[pltpu.VMEM((D, tm), _F32), pltpu.VMEM((te, tm), _BF16)],
        compiler_params=_cparams("parallel", "arbitrary"),
        name="peer_experts",
    )(xn, down_bf16, upt_bf16, c1, e1, r2, e2, h, gf)


def _block_diag(w):
    n, d, _ = w.shape
    eye = jnp.eye(n, dtype=w.dtype)
    return (eye[:, None, :, None] * w[:, :, None, :]).reshape(n * d, n * d)


def _tile(n, want):
    t = min(n, want)
    assert n % t == 0, (n, want)
    return t


def kernel(x, norm1_g, w_in, lambda_q1, lambda_k1, lambda_q2, lambda_k2, subln_g, conv_w, conv_b,
           w_rec_gate, b_rec_gate, w_in_gate, b_in_gate, lru_lambda, w_out, norm2_g, w_query,
           sub_keys_1, sub_keys_2, expert_down, expert_up, norm_f_g):
    B, S, D = x.shape
    T = B * S
    assert B == 1, "sequence mixing kernels assume a single sequence"
    assert w_in.shape[0] == 1, "the final norm is fused into the (single) layer's PEER kernel"
    l = 0
    h = x.reshape(T, D)
    row = lambda v: v.reshape(1, -1)
    lambda_init = 0.8 - 0.6 * math.exp(-0.3 * l)
    q, k, v, xr, gate = _in_proj(h, row(norm1_g[l]), w_in[l].astype(_BF16), _tile(T, 512))
    attn = _attention(q, k, v, row(lambda_q1[l]), row(lambda_k1[l]), row(lambda_q2[l]),
                      row(lambda_k2[l]), row(subln_g[l]), lambda_init, _tile(T, 512), 4)
    w_gates = jnp.concatenate([_block_diag(w_rec_gate[l]), _block_diag(w_in_gate[l])],
                              axis=1).astype(_BF16)
    lru = _rglru(xr, gate, conv_w[l], row(conv_b[l]), w_gates, row(b_rec_gate[l]),
                 row(b_in_gate[l]), row(lru_lambda[l]), _tile(T, 256))
    h_mid, xnt, c1, e1, r2, e2 = _mid(
        attn, lru, h, w_out[l].astype(_BF16), row(norm2_g[l]),
        jnp.transpose(w_query[l]).astype(_BF16), sub_keys_1[l].astype(_BF16),
        sub_keys_2[l].astype(_BF16), _tile(T, 512))
    out = _peer(xnt, expert_down[l].astype(_BF16), jnp.transpose(expert_up[l]).astype(_BF16),
                c1, e1, r2, e2, h_mid, row(norm_f_g), _tile(T, 512), 1024)
    return out.reshape(B, S, D)
```

```python
import functools
import math

import jax
import jax.numpy as jnp
from jax import lax
from jax.experimental import pallas as pl
from jax.experimental.pallas import tpu as pltpu

CHUNK = 64
ATTN_HEAD_DIM = 64
ATTN_HEADS = 4
ATTN_WIDTH = 2 * ATTN_HEAD_DIM * ATTN_HEADS
LRU_BLOCKS = 8
CONV_WIDTH = 4
LRU_C = 8.0
PEER_HEADS = 8
PEER_N_KEYS = 128
PEER_HALF = 128
PEER_TOPK = 16
EPS = 1e-6
NEG_INF = -1e30

LANES = 128
SUBLANES = 8
BF16_VREG_LANES = 256
VMEM_LIMIT_BYTES = 48 * 1024 * 1024

_F32 = jnp.float32
_BF16 = jnp.bfloat16
_NT_DIMS = (((1,), (1,)), ((), ()))


def _cparams(*sem, flags=None):
    return pltpu.CompilerParams(dimension_semantics=sem, vmem_limit_bytes=VMEM_LIMIT_BYTES, flags=flags)


def _rms(x, g):
    return x * lax.rsqrt(jnp.mean(x * x, axis=-1, keepdims=True) + EPS) * g


def _in_proj_kernel(x_ref, g_ref, w_ref, q_ref, k_ref, v_ref, xr_ref, gate_ref):
    n = _rms(x_ref[...], g_ref[...]).astype(_BF16)
    hw = 2 * ATTN_HEAD_DIM

    def proj(c0, width):
        return jnp.dot(n, w_ref[:, c0:c0 + width], preferred_element_type=_F32)

    scale = ATTN_HEAD_DIM ** -0.5 * math.log2(math.e)
    ones_col = (lax.broadcasted_iota(jnp.int32, (x_ref.shape[0], hw), 1) == 0).astype(_BF16)
    for h in range(ATTN_HEADS):
        q_ref[h] = (proj(h * hw, hw) * scale).astype(_BF16)
        k_ref[h] = proj(ATTN_WIDTH + h * hw, hw).astype(_BF16)
        v_ref[h, :, 0:hw] = proj(2 * ATTN_WIDTH + h * hw, hw).astype(_BF16)
        v_ref[h, :, hw:2 * hw] = ones_col
    lru_w = xr_ref.shape[-1]
    xr_ref[...] = proj(3 * ATTN_WIDTH, lru_w)
    gate_ref[...] = proj(3 * ATTN_WIDTH + lru_w, lru_w)


def _in_proj(x2, g, w_bf16, tm):
    T, D = x2.shape
    lru_w = (w_bf16.shape[1] - 3 * ATTN_WIDTH) // 2
    hw = 2 * ATTN_HEAD_DIM
    head_shape = jax.ShapeDtypeStruct((ATTN_HEADS, T, hw), _BF16)
    head_spec = pl.BlockSpec((ATTN_HEADS, tm, hw), lambda i: (0, i, 0))
    v_shape = jax.ShapeDtypeStruct((ATTN_HEADS, T, 2 * hw), _BF16)
    v_spec = pl.BlockSpec((ATTN_HEADS, tm, 2 * hw), lambda i: (0, i, 0))
    row_spec = pl.BlockSpec((tm, lru_w), lambda i: (i, 0))
    return pl.pallas_call(
        _in_proj_kernel,
        grid=(T // tm,),
        in_specs=[pl.BlockSpec((tm, D), lambda i: (i, 0)),
                  pl.BlockSpec((1, D), lambda i: (0, 0)),
                  pl.BlockSpec(w_bf16.shape, lambda i: (0, 0))],
        out_specs=[head_spec, head_spec, v_spec, row_spec, row_spec],
        out_shape=[head_shape, head_shape, v_shape,
                   jax.ShapeDtypeStruct((T, lru_w), _F32), jax.ShapeDtypeStruct((T, lru_w), _F32)],
        compiler_params=_cparams("parallel"),
        name="in_proj",
    )(x2, g, w_bf16)


def _attn_kernel(lq1_ref, lk1_ref, lq2_ref, lk2_ref, sg_ref, q_ref, k_ref, v_ref, o_ref,
                 m_ref, acc_ref, *, blk, nsub, lambda_init):
    i = pl.program_id(1)
    hd = ATTN_HEAD_DIM
    hw = 2 * hd
    lane = lax.broadcasted_iota(jnp.int32, (blk, hw), 1)
    q_maps = []
    for sb in range(nsub):
        q = q_ref[sb * blk:(sb + 1) * blk, :]
        zero = jnp.zeros_like(q)
        q_maps.append((jnp.where(lane < hd, q, zero), jnp.where(lane >= hd, q, zero)))

    m_ref[...] = jnp.full(m_ref.shape, NEG_INF, _F32)
    acc_ref[...] = jnp.zeros(acc_ref.shape, _F32)

    def chain_update(sb, mp, kj, vj, visible):
        s = lax.dot_general(q_maps[sb][mp], kj, _NT_DIMS, preferred_element_type=_F32)
        if visible is not None:
            s = jnp.where(visible, s, NEG_INF)
        idx = 2 * sb + mp
        m_prev = m_ref[idx]
        m_next = jnp.maximum(m_prev, jnp.max(s, axis=1, keepdims=True))
        alpha = jnp.exp2(m_prev - m_next)
        e = jnp.exp2(s - jnp.tile(m_next, (1, blk // LANES)))
        m_ref[idx] = m_next
        pv = jnp.dot(e.astype(_BF16), vj, preferred_element_type=_F32)
        acc_ref[idx] = jnp.tile(alpha, (1, 2 * hw // LANES)) * acc_ref[idx] + pv

    def load_kv(j):
        row0 = pl.multiple_of(j * blk, blk)
        return k_ref[pl.ds(row0, blk), :], v_ref[pl.ds(row0, blk), :]

    def body(j, carry):
        kj, vj = load_kv(j)
        for sb in range(nsub):
            for mp in range(2):
                chain_update(sb, mp, kj, vj, None)
        return carry

    lax.fori_loop(0, nsub * i, body, 0)

    qc = lax.broadcasted_iota(jnp.int32, (blk, blk), 0) // CHUNK
    kc = lax.broadcasted_iota(jnp.int32, (blk, blk), 1) // CHUNK
    diagonal = qc >= kc
    for t in range(nsub):
        kj, vj = load_kv(nsub * i + t)
        for sb in range(t, nsub):
            for mp in range(2):
                chain_update(sb, mp, kj, vj, diagonal if sb == t else None)

    lam = (jnp.exp(jnp.sum(lq1_ref[...] * lk1_ref[...], keepdims=True))
           - jnp.exp(jnp.sum(lq2_ref[...] * lk2_ref[...], keepdims=True)) + lambda_init)
    for sb in range(nsub):
        a0, a1 = acc_ref[2 * sb], acc_ref[2 * sb + 1]
        o = a0[:, 0:hw] / a0[:, hw:hw + 1] - lam * (a1[:, 0:hw] / a1[:, hw:hw + 1])
        o = o * lax.rsqrt(jnp.mean(o * o, axis=-1, keepdims=True) + EPS)
        o_ref[sb * blk:(sb + 1) * blk, :] = (o * sg_ref[...] * (1.0 - lambda_init)).astype(o_ref.dtype)


def _attention(q, k, v, lq1, lk1, lq2, lk2, subln_g, lambda_init, blk, nsub):
    H, T, hw = q.shape
    bq = blk * nsub
    vec = pl.BlockSpec((1, ATTN_HEAD_DIM), lambda h, i: (0, 0))
    kernel = functools.partial(_attn_kernel, blk=blk, nsub=nsub, lambda_init=lambda_init)
    return pl.pallas_call(
        kernel,
        grid=(H, T // bq),
        in_specs=[vec, vec, vec, vec,
                  pl.BlockSpec((1, hw), lambda h, i: (0, 0)),
                  pl.BlockSpec((None, bq, hw), lambda h, i: (h, i, 0)),
                  pl.BlockSpec((None, T, hw), lambda h, i: (h, 0, 0)),
                  pl.BlockSpec((None, T, 2 * hw), lambda h, i: (h, 0, 0))],
        out_specs=pl.BlockSpec((bq, hw), lambda h, i: (i, h)),
        out_shape=jax.ShapeDtypeStruct((T, H * hw), _BF16),
        scratch_shapes=[pltpu.VMEM((2 * nsub, blk, LANES), _F32),
                        pltpu.VMEM((2 * nsub, blk, 2 * hw), _F32)],
        compiler_params=_cparams("parallel", "parallel"),
        name="diff_attention",
    )(lq1, lk1, lq2, lk2, subln_g, q, k, v)


def _shift_rows(x, s, fill):
    rolled = pltpu.roll(x, s, axis=0)
    row = lax.broadcasted_iota(jnp.int32, x.shape, 0)
    return jnp.where(row >= s, rolled, fill)


def _rglru_kernel(xr_ref, gate_ref, cw_ref, cb_ref, wg_ref, ba_ref, bx_ref, lam_ref, o_ref,
                  xbuf_ref, hprev_ref, *, tl):
    halo = SUBLANES
    W = xr_ref.shape[-1]

    @pl.when(pl.program_id(0) == 0)
    def _():
        xbuf_ref[0:halo, :] = jnp.zeros((halo, W), _F32)
        hprev_ref[...] = jnp.zeros(hprev_ref.shape, _F32)

    x = xr_ref[...]
    xbuf_ref[halo:halo + tl, :] = x
    y = cb_ref[...] + x * cw_ref[CONV_WIDTH - 1:CONV_WIDTH, :]
    for t in range(CONV_WIDTH - 1):
        back = CONV_WIDTH - 1 - t
        y = y + xbuf_ref[halo - back:halo - back + tl, :] * cw_ref[t:t + 1, :]
    xbuf_ref[0:halo, :] = x[tl - halo:, :]

    gates = jnp.dot(y.astype(_BF16), wg_ref[...], preferred_element_type=_F32)
    r = jax.nn.sigmoid(gates[:, :W] + ba_ref[...])
    ig = jax.nn.sigmoid(gates[:, W:] + bx_ref[...])
    log_a = -LRU_C * r * jax.nn.softplus(-lam_ref[...])
    a = jnp.exp(log_a)
    one_minus_a2 = -jnp.tanh(log_a) * (1.0 + a * a)
    b = jnp.sqrt(jnp.maximum(one_minus_a2, 1e-12)) * (ig * y)

    s = 1
    while s < tl:
        a_sh = _shift_rows(a, s, 1.0)
        b_sh = _shift_rows(b, s, 0.0)
        b = a * b_sh + b
        a = a * a_sh
        s *= 2
    h = b + a * hprev_ref[0:1, :]
    hprev_ref[...] = jnp.broadcast_to(h[tl - 1:tl, :], hprev_ref.shape)
    o_ref[...] = (h * jax.nn.gelu(gate_ref[...])).astype(o_ref.dtype)


def _rglru(xr, gate, conv_w, conv_b, wg_bf16, b_a, b_x, lru_lambda, tl):
    T, W = xr.shape
    row = pl.BlockSpec((tl, W), lambda i: (i, 0))
    vec = pl.BlockSpec((1, W), lambda i: (0, 0))
    return pl.pallas_call(
        functools.partial(_rglru_kernel, tl=tl),
        grid=(T // tl,),
        in_specs=[row, row,
                  pl.BlockSpec(conv_w.shape, lambda i: (0, 0)),
                  vec,
                  pl.BlockSpec(wg_bf16.shape, lambda i: (0, 0)),
                  vec, vec, vec],
        out_specs=row,
        out_shape=jax.ShapeDtypeStruct((T, W), _BF16),
        scratch_shapes=[pltpu.VMEM((tl + SUBLANES, W), _F32),
                        pltpu.VMEM((SUBLANES, W), _F32)],
        compiler_params=_cparams("arbitrary"),
        name="rglru",
    )(xr, gate, conv_w, conv_b, wg_bf16, b_a, b_x, lru_lambda)


def _oddeven_mergesort_pairs(n):
    pairs = []
    p = 1
    while p < n:
        k = p
        while k >= 1:
            for j in range(k % p, n - k, 2 * k):
                for i in range(min(k, n - j - k)):
                    if (i + j) // (2 * p) == (i + j + k) // (2 * p):
                        pairs.append((i + j, i + j + k))
            k //= 2
        p *= 2
    return pairs


_SORT16 = _oddeven_mergesort_pairs(PEER_TOPK)


def _sort_desc(xs):
    xs = list(xs)
    for lo, hi in _SORT16:
        a, b = xs[lo], xs[hi]
        xs[lo], xs[hi] = jnp.maximum(a, b), jnp.minimum(a, b)
    return xs


def _top_of_union(a_sorted, b_sorted):
    n = len(a_sorted)
    return [jnp.maximum(a_sorted[i], b_sorted[n - 1 - i]) for i in range(n)]


def _bitonic_merge_desc(xs):
    xs = list(xs)
    n = len(xs)
    d = n // 2
    while d >= 1:
        for i in range(n):
            if i & d == 0:
                a, b = xs[i], xs[i + d]
                xs[i], xs[i + d] = jnp.maximum(a, b), jnp.minimum(a, b)
        d //= 2
    return xs


def _prefix_count(pred, vals):
    assert len(vals) == PEER_TOPK == 16
    bits = []
    step = 8
    while step >= 1:
        cands = [vals[base + step - 1] for base in range(0, 16, 2 * step)]
        for b in reversed(bits):
            cands = [jnp.where(b, hi, lo) for lo, hi in zip(cands[0::2], cands[1::2])]
        bits.append(pred(cands[0]))
        step //= 2
    count = jnp.where(pred(vals[15]), 1.0, 0.0)
    for b, w in zip(bits, (8.0, 4.0, 2.0, 1.0)):
        count = count + jnp.where(b, w, 0.0)
    return count


def _pair_threshold(t1, t2):
    k = PEER_TOPK
    rows = [[t1[a] + t2[b] for b in range(k // (a + 1))] for a in range(k)]
    neg = jnp.full_like(t1[0], NEG_INF)
    g1 = _sort_desc(rows[1] + rows[2] + rows[4])
    g2 = _sort_desc(rows[3] + rows[5] + rows[6] + rows[7]
                    + [rows[a][0] for a in range(8, 14)])
    g3 = _sort_desc([rows[14][0], rows[15][0]] + [neg] * (k - 2))
    top = _bitonic_merge_desc(_top_of_union(rows[0], g1))
    top = _bitonic_merge_desc(_top_of_union(top, g2))
    top = _top_of_union(top, g3)
    return top, rows[0][0]


def _mid_kernel(attn_ref, lru_ref, x_ref, wo_ref, g2_ref, wqt_ref, k1_ref, k2_ref,
                h_ref, xn_ref, c1_ref, e1_ref, r2_ref, e2_ref, qt_ref, sorted_ref, top2_ref, stat_ref):
    aw = attn_ref.shape[-1]
    h = (x_ref[...]
         + jnp.dot(attn_ref[...], wo_ref[0:aw, :], preferred_element_type=_F32)
         + jnp.dot(lru_ref[...], wo_ref[aw:, :], preferred_element_type=_F32))
    h_ref[...] = h
    xn = _rms(h, g2_ref[...]).astype(_BF16)
    xn_ref[...] = xn
    qt_ref[...] = lax.dot_general(wqt_ref[...], xn, _NT_DIMS, preferred_element_type=_F32)

    nslab = PEER_N_KEYS // SUBLANES
    tm = qt_ref.shape[-1]
    assert PEER_HEADS == SUBLANES, "phase 2 puts one head on each sublane"

    def scores(hd):
        base = pl.multiple_of(hd * (2 * PEER_HALF), 2 * PEER_HALF)
        q1 = qt_ref[pl.ds(base, PEER_HALF), :].astype(_BF16)
        q2 = qt_ref[pl.ds(base + PEER_HALF, PEER_HALF), :].astype(_BF16)
        return (jnp.dot(k1_ref[...], q1, preferred_element_type=_F32),
                jnp.dot(k2_ref[...], q2, preferred_element_type=_F32))

    def sort_head(hd, carry):
        for side, s in enumerate(scores(hd)):
            srt = _sort_desc([s[j * SUBLANES:(j + 1) * SUBLANES, :] for j in range(nslab)])
            for b in range(PEER_TOPK):
                for sl in range(SUBLANES):
                    sorted_ref[side, b, sl, pl.ds(hd, 1), :] = srt[b][sl:sl + 1, :]
        return carry

    lax.fori_loop(0, PEER_HEADS, sort_head, 0)

    def merged_top16(side):
        lists = [[sorted_ref[side, b, sl] for b in range(PEER_TOPK)] for sl in range(SUBLANES)]
        while len(lists) > 1:
            lists = [_bitonic_merge_desc(_top_of_union(a, b)) for a, b in zip(lists[0::2], lists[1::2])]
        return lists[0]

    t1, t2 = merged_top16(0), merged_top16(1)
    top, m = _pair_threshold(t1, t2)
    z = functools.reduce(jnp.add, [jnp.exp(t - m) for t in top])
    for b in range(PEER_TOPK):
        top2_ref[b] = t2[b]
    stat_ref[0] = functools.reduce(jnp.minimum, top)
    stat_ref[1] = t1[0]
    stat_ref[2] = 1.0 / z

    def tables(hd, carry):
        def head_row(ref, k):
            return jnp.broadcast_to(ref[k, pl.ds(hd, 1), :], (SUBLANES, tm))

        s1, s2 = scores(hd)
        t2h = [head_row(top2_ref, b) for b in range(PEER_TOPK)]
        tau, m1, inv_z = head_row(stat_ref, 0), head_row(stat_ref, 1), head_row(stat_ref, 2)
        ranks, e2s = [], []
        for j in range(nslab):
            rows = slice(j * SUBLANES, (j + 1) * SUBLANES)
            s1j, s2j = s1[rows, :], s2[rows, :]
            cnt = _prefix_count(lambda v: s1j + v >= tau, t2h)
            rank = _prefix_count(lambda v: v > s2j, t2h)
            c1_ref[hd, rows, :] = cnt
            e1_ref[hd, rows, :] = jnp.exp(s1j - m1)
            ranks.append(rank)
            e2s.append(jnp.exp(s2j - t2h[0]) * inv_z)
        r2_ref[hd] = jnp.concatenate(ranks, axis=0).astype(_BF16)
        e2_ref[hd] = jnp.concatenate(e2s, axis=0).astype(_BF16)
        return carry

    lax.fori_loop(0, PEER_HEADS, tables, 0)


def _mid(attn, lru, x2, wo_bf16, g2, wqt_bf16, k1_bf16, k2_bf16, tm):
    T, D = x2.shape
    aw, lw = attn.shape[1], lru.shape[1]
    tab_f32 = jax.ShapeDtypeStruct((PEER_HEADS, PEER_N_KEYS, T), _F32)
    tab_bf16 = jax.ShapeDtypeStruct((PEER_HEADS, PEER_N_KEYS, T), _BF16)
    tab_spec = pl.BlockSpec((PEER_HEADS, PEER_N_KEYS, tm), lambda i: (0, 0, i))
    const = lambda a: pl.BlockSpec(a.shape, lambda i: (0,) * a.ndim)
    return pl.pallas_call(
        _mid_kernel,
        grid=(T // tm,),
        in_specs=[pl.BlockSpec((tm, aw), lambda i: (i, 0)),
                  pl.BlockSpec((tm, lw), lambda i: (i, 0)),
                  pl.BlockSpec((tm, D), lambda i: (i, 0)),
                  const(wo_bf16), const(g2), const(wqt_bf16), const(k1_bf16), const(k2_bf16)],
        out_specs=[pl.BlockSpec((tm, D), lambda i: (i, 0)),
                   pl.BlockSpec((tm, D), lambda i: (i, 0)),
                   tab_spec, tab_spec, tab_spec, tab_spec],
        out_shape=[jax.ShapeDtypeStruct((T, D), _F32),
                   jax.ShapeDtypeStruct((T, D), _BF16),
                   tab_f32, tab_f32, tab_bf16, tab_bf16],
        scratch_shapes=[pltpu.VMEM((wqt_bf16.shape[0], tm), _F32),
                        pltpu.VMEM((2, PEER_TOPK, SUBLANES, PEER_HEADS, tm), _F32),
                        pltpu.VMEM((PEER_TOPK, PEER_HEADS, tm), _F32),
                        pltpu.VMEM((3, PEER_HEADS, tm), _F32)],
        compiler_params=_cparams("parallel"),
        name="mid_proj_topk",
    )(attn, lru, x2, wo_bf16, g2, wqt_bf16, k1_bf16, k2_bf16)


def _gelu_tanh(x):
    c1 = 2.0 * math.sqrt(2.0 / math.pi)
    c2 = c1 * 0.044715
    return x / (1.0 + jnp.exp(x * (-c1 - c2 * (x * x))))


def _peer_kernel(xn_ref, down_ref, upt_ref, c1_ref, e1_ref, r2_ref, e2_ref, h_ref, gf_ref,
                 o_ref, acc_ref, hid_ref, *, te):
    j = pl.program_id(1)
    nk = PEER_N_KEYS
    tm = xn_ref.shape[0]
    W = BF16_VREG_LANES
    R = 64
    nb = R // SUBLANES
    mb = 2 * nk

    @pl.when(j == 0)
    def _():
        acc_ref[...] = jnp.zeros(acc_ref.shape, _F32)

    row0 = pl.multiple_of(j * (te // nk), SUBLANES)
    zero = jnp.zeros((nb, SUBLANES, W), _BF16)
    xn = xn_ref[...]
    for m in range(te // mb):
        act = lax.dot_general(down_ref[m * mb:(m + 1) * mb, :], xn, _NT_DIMS,
                              preferred_element_type=_F32)
        for c in range(tm // W):
            lanes = slice(c * W, (c + 1) * W)
            for a in range(mb // nk):
                i1 = m * (mb // nk) + a
                cnts, e1s = [], []
                for hd in range(PEER_HEADS):
                    grp = pl.ds(row0 + (i1 // SUBLANES) * SUBLANES, SUBLANES)
                    c1_row = c1_ref[hd, grp, lanes][i1 % SUBLANES:i1 % SUBLANES + 1, :]
                    e1_row = e1_ref[hd, grp, lanes][i1 % SUBLANES:i1 % SUBLANES + 1, :]
                    cnts.append(jnp.broadcast_to(c1_row, (SUBLANES, W)).astype(_BF16)[None])
                    e1s.append(jnp.broadcast_to(e1_row, (SUBLANES, W)).astype(_BF16)[None])
                for rb in range(nk // R):
                    gate = zero
                    for hd in range(PEER_HEADS):
                        r2 = r2_ref[hd, rb * R:(rb + 1) * R, lanes].reshape(nb, SUBLANES, W)
                        e2 = e2_ref[hd, rb * R:(rb + 1) * R, lanes].reshape(nb, SUBLANES, W)
                        gate = gate + jnp.where(r2 < cnts[hd], e2, zero) * e1s[hd]
                    g = _gelu_tanh(act[a * nk + rb * R:a * nk + (rb + 1) * R, lanes].astype(_BF16))
                    rows = slice(i1 * nk + rb * R, i1 * nk + (rb + 1) * R)
                    hid_ref[rows, lanes] = (g.reshape(nb, SUBLANES, W) * gate).reshape(R, W)
    acc_ref[...] += jnp.dot(upt_ref[...], hid_ref[...], preferred_element_type=_F32)

    @pl.when(j == pl.num_programs(1) - 1)
    def _():
        o_ref[...] = _rms(h_ref[...] + jnp.transpose(acc_ref[...]), gf_ref[...])


def _peer(xn, down_bf16, upt_bf16, c1, e1, r2, e2, h, gf, tm, te):
    T, D = xn.shape
    E = down_bf16.shape[0]
    nk = PEER_N_KEYS
    assert te % (nk * SUBLANES) == 0, "an expert tile spans whole sublane groups of first-key rows"
    tab_spec = pl.BlockSpec((PEER_HEADS, nk, tm), lambda i, j: (0, 0, i))
    return pl.pallas_call(
        functools.partial(_peer_kernel, te=te),
        grid=(T // tm, E // te),
        in_specs=[pl.BlockSpec((tm, D), lambda i, j: (i, 0)),
                  pl.BlockSpec((te, D), lambda i, j: (j, 0)),
                  pl.BlockSpec((D, te), lambda i, j: (0, j)),
                  tab_spec, tab_spec, tab_spec, tab_spec,
                  pl.BlockSpec((tm, D), lambda i, j: (i, 0)),
                  pl.BlockSpec((1, D), lambda i, j: (0, 0))],
        out_specs=pl.BlockSpec((tm, D), lambda i, j: (i, 0)),
        out_shape=jax.ShapeDtypeStruct((T, D), _F32),
        scratch_shapes=[pltpu.VMEM((D, tm), _F32), pltpu.VMEM((te, tm), _BF16)],
        compiler_params=_cparams("parallel", "arbitrary"),
        name="peer_experts",
    )(xn, down_bf16, upt_bf16, c1, e1, r2, e2, h, gf)


def _block_diag(w):
    n, d, _ = w.shape
    eye = jnp.eye(n, dtype=w.dtype)
    return (eye[:, None, :, None] * w[:, :, None, :]).reshape(n * d, n * d)


def _tile(n, want):
    t = min(n, want)
    assert n % t == 0, (n, want)
    return t


def kernel(x, norm1_g, w_in, lambda_q1, lambda_k1, lambda_q2, lambda_k2, subln_g, conv_w, conv_b,
           w_rec_gate, b_rec_gate, w_in_gate, b_in_gate, lru_lambda, w_out, norm2_g, w_query,
           sub_keys_1, sub_keys_2, expert_down, expert_up, norm_f_g):
    B, S, D = x.shape
    T = B * S
    assert B == 1, "sequence mixing kernels assume a single sequence"
    assert w_in.shape[0] == 1, "the final norm is fused into the (single) layer's PEER kernel"
    l = 0
    h = x.reshape(T, D)
    row = lambda v: v.reshape(1, -1)
    lambda_init = 0.8 - 0.6 * math.exp(-0.3 * l)
    q, k, v, xr, gate = _in_proj(h, row(norm1_g[l]), w_in[l].astype(_BF16), _tile(T, 512))
    attn = _attention(q, k, v, row(lambda_q1[l]), row(lambda_k1[l]), row(lambda_q2[l]),
                      row(lambda_k2[l]), row(subln_g[l]), lambda_init, _tile(T, 512), 4)
    w_gates = jnp.concatenate([_block_diag(w_rec_gate[l]), _block_diag(w_in_gate[l])],
                              axis=1).astype(_BF16)
    lru = _rglru(xr, gate, conv_w[l], row(conv_b[l]), w_gates, row(b_rec_gate[l]),
                 row(b_in_gate[l]), row(lru_lambda[l]), _tile(T, 256))
    h_mid, xnt, c1, e1, r2, e2 = _mid(
        attn, lru, h, w_out[l].astype(_BF16), row(norm2_g[l]),
        jnp.transpose(w_query[l]).astype(_BF16), sub_keys_1[l].astype(_BF16),
        sub_keys_2[l].astype(_BF16), _tile(T, 512))
    out = _peer(xnt, expert_down[l].astype(_BF16), jnp.transpose(expert_up[l]).astype(_BF16),
                c1, e1, r2, e2, h_mid, row(norm_f_g), _tile(T, 512), 2048)
    return out.reshape(B, S, D)
```

```python
import functools
import math

import jax
import jax.numpy as jnp
from jax import lax
from jax.experimental import pallas as pl
from jax.experimental.pallas import tpu as pltpu

CHUNK = 64
ATTN_HEAD_DIM = 64
ATTN_HEADS = 4
ATTN_WIDTH = 2 * ATTN_HEAD_DIM * ATTN_HEADS
LRU_BLOCKS = 8
CONV_WIDTH = 4
LRU_C = 8.0
PEER_HEADS = 8
PEER_N_KEYS = 128
PEER_HALF = 128
PEER_TOPK = 16
EPS = 1e-6
NEG_INF = -1e30

LANES = 128
SUBLANES = 8
BF16_VREG_LANES = 256
VMEM_LIMIT_BYTES = 48 * 1024 * 1024

_F32 = jnp.float32
_BF16 = jnp.bfloat16
_NT_DIMS = (((1,), (1,)), ((), ()))


def _cparams(*sem, flags=None):
    return pltpu.CompilerParams(dimension_semantics=sem, vmem_limit_bytes=VMEM_LIMIT_BYTES, flags=flags)


def _rms(x, g):
    return x * lax.rsqrt(jnp.mean(x * x, axis=-1, keepdims=True) + EPS) * g


def _in_proj_kernel(x_ref, g_ref, w_ref, q_ref, k_ref, v_ref, xr_ref, gate_ref):
    n = _rms(x_ref[...], g_ref[...]).astype(_BF16)
    hw = 2 * ATTN_HEAD_DIM

    def proj(c0, width):
        return jnp.dot(n, w_ref[:, c0:c0 + width], preferred_element_type=_F32)

    scale = ATTN_HEAD_DIM ** -0.5 * math.log2(math.e)
    ones_col = (lax.broadcasted_iota(jnp.int32, (x_ref.shape[0], hw), 1) == 0).astype(_BF16)
    q_all = proj(0, ATTN_WIDTH) * scale
    k_all = proj(ATTN_WIDTH, ATTN_WIDTH)
    v_all = proj(2 * ATTN_WIDTH, ATTN_WIDTH)
    for h in range(ATTN_HEADS):
        cols = slice(h * hw, (h + 1) * hw)
        q_ref[h] = q_all[:, cols].astype(_BF16)
        k_ref[h] = k_all[:, cols].astype(_BF16)
        v_ref[h, :, 0:hw] = v_all[:, cols].astype(_BF16)
        v_ref[h, :, hw:2 * hw] = ones_col
    lru_w = xr_ref.shape[-1]
    xr_ref[...] = proj(3 * ATTN_WIDTH, lru_w)
    gate_ref[...] = proj(3 * ATTN_WIDTH + lru_w, lru_w)


def _in_proj(x2, g, w_bf16, tm):
    T, D = x2.shape
    lru_w = (w_bf16.shape[1] - 3 * ATTN_WIDTH) // 2
    hw = 2 * ATTN_HEAD_DIM
    head_shape = jax.ShapeDtypeStruct((ATTN_HEADS, T, hw), _BF16)
    head_spec = pl.BlockSpec((ATTN_HEADS, tm, hw), lambda i: (0, i, 0))
    v_shape = jax.ShapeDtypeStruct((ATTN_HEADS, T, 2 * hw), _BF16)
    v_spec = pl.BlockSpec((ATTN_HEADS, tm, 2 * hw), lambda i: (0, i, 0))
    row_spec = pl.BlockSpec((tm, lru_w), lambda i: (i, 0))
    return pl.pallas_call(
        _in_proj_kernel,
        grid=(T // tm,),
        in_specs=[pl.BlockSpec((tm, D), lambda i: (i, 0)),
                  pl.BlockSpec((1, D), lambda i: (0, 0)),
                  pl.BlockSpec(w_bf16.shape, lambda i: (0, 0))],
        out_specs=[head_spec, head_spec, v_spec, row_spec, row_spec],
        out_shape=[head_shape, head_shape, v_shape,
                   jax.ShapeDtypeStruct((T, lru_w), _F32), jax.ShapeDtypeStruct((T, lru_w), _F32)],
        compiler_params=_cparams("parallel"),
        name="in_proj",
    )(x2, g, w_bf16)


def _attn_kernel(lq1_ref, lk1_ref, lq2_ref, lk2_ref, sg_ref, q_ref, k_ref, v_ref, o_ref,
                 m_ref, acc_ref, *, blk, nsub, lambda_init):
    i = pl.program_id(1)
    hd = ATTN_HEAD_DIM
    hw = 2 * hd
    lane = lax.broadcasted_iota(jnp.int32, (blk, hw), 1)
    q_maps = []
    for sb in range(nsub):
        q = q_ref[sb * blk:(sb + 1) * blk, :]
        zero = jnp.zeros_like(q)
        q_maps.append((jnp.where(lane < hd, q, zero), jnp.where(lane >= hd, q, zero)))

    m_ref[...] = jnp.full(m_ref.shape, NEG_INF, _F32)
    acc_ref[...] = jnp.zeros(acc_ref.shape, _F32)

    def chain_update(sb, mp, kj, vj, visible):
        s = lax.dot_general(q_maps[sb][mp], kj, _NT_DIMS, preferred_element_type=_F32)
        if visible is not None:
            s = jnp.where(visible, s, NEG_INF)
        idx = 2 * sb + mp
        m_prev = m_ref[idx]
        m_next = jnp.maximum(m_prev, jnp.max(s, axis=1, keepdims=True))
        alpha = jnp.exp2(m_prev - m_next)
        e = jnp.exp2(s - jnp.tile(m_next, (1, blk // LANES)))
        m_ref[idx] = m_next
        pv = jnp.dot(e.astype(_BF16), vj, preferred_element_type=_F32)
        acc_ref[idx] = jnp.tile(alpha, (1, 2 * hw // LANES)) * acc_ref[idx] + pv

    def load_kv(j):
        row0 = pl.multiple_of(j * blk, blk)
        return k_ref[pl.ds(row0, blk), :], v_ref[pl.ds(row0, blk), :]

    kv_unroll = 4
    assert nsub % kv_unroll == 0

    def body(j, carry):
        for u in range(kv_unroll):
            kj, vj = load_kv(j * kv_unroll + u)
            for sb in range(nsub):
                for mp in range(2):
                    chain_update(sb, mp, kj, vj, None)
        return carry

    lax.fori_loop(0, (nsub // kv_unroll) * i, body, 0)

    qc = lax.broadcasted_iota(jnp.int32, (blk, blk), 0) // CHUNK
    kc = lax.broadcasted_iota(jnp.int32, (blk, blk), 1) // CHUNK
    diagonal = qc >= kc
    for t in range(nsub):
        kj, vj = load_kv(nsub * i + t)
        for sb in range(t, nsub):
            for mp in range(2):
                chain_update(sb, mp, kj, vj, diagonal if sb == t else None)

    lam = (jnp.exp(jnp.sum(lq1_ref[...] * lk1_ref[...], keepdims=True))
           - jnp.exp(jnp.sum(lq2_ref[...] * lk2_ref[...], keepdims=True)) + lambda_init)
    for sb in range(nsub):
        a0, a1 = acc_ref[2 * sb], acc_ref[2 * sb + 1]
        o = a0[:, 0:hw] / a0[:, hw:hw + 1] - lam * (a1[:, 0:hw] / a1[:, hw:hw + 1])
        o = o * lax.rsqrt(jnp.mean(o * o, axis=-1, keepdims=True) + EPS)
        o_ref[sb * blk:(sb + 1) * blk, :] = (o * sg_ref[...] * (1.0 - lambda_init)).astype(o_ref.dtype)


def _attention(q, k, v, lq1, lk1, lq2, lk2, subln_g, lambda_init, blk, nsub):
    H, T, hw = q.shape
    bq = blk * nsub
    vec = pl.BlockSpec((1, ATTN_HEAD_DIM), lambda h, i: (0, 0))
    kernel = functools.partial(_attn_kernel, blk=blk, nsub=nsub, lambda_init=lambda_init)
    return pl.pallas_call(
        kernel,
        grid=(H, T // bq),
        in_specs=[vec, vec, vec, vec,
                  pl.BlockSpec((1, hw), lambda h, i: (0, 0)),
                  pl.BlockSpec((None, bq, hw), lambda h, i: (h, i, 0)),
                  pl.BlockSpec((None, T, hw), lambda h, i: (h, 0, 0)),
                  pl.BlockSpec((None, T, 2 * hw), lambda h, i: (h, 0, 0))],
        out_specs=pl.BlockSpec((bq, hw), lambda h, i: (i, h)),
        out_shape=jax.ShapeDtypeStruct((T, H * hw), _BF16),
        scratch_shapes=[pltpu.VMEM((2 * nsub, blk, LANES), _F32),
                        pltpu.VMEM((2 * nsub, blk, 2 * hw), _F32)],
        compiler_params=_cparams("parallel", "parallel"),
        name="diff_attention",
    )(lq1, lk1, lq2, lk2, subln_g, q, k, v)


def _shift_rows(x, s, fill):
    rolled = pltpu.roll(x, s, axis=0)
    row = lax.broadcasted_iota(jnp.int32, x.shape, 0)
    return jnp.where(row >= s, rolled, fill)


def _rglru_kernel(xr_ref, gate_ref, cw_ref, cb_ref, wg_ref, ba_ref, bx_ref, lam_ref, o_ref,
                  xbuf_ref, hprev_ref, *, tl):
    halo = SUBLANES
    W = xr_ref.shape[-1]

    @pl.when(pl.program_id(0) == 0)
    def _():
        xbuf_ref[0:halo, :] = jnp.zeros((halo, W), _F32)
        hprev_ref[...] = jnp.zeros(hprev_ref.shape, _F32)

    x = xr_ref[...]
    xbuf_ref[halo:halo + tl, :] = x
    y = cb_ref[...] + x * cw_ref[CONV_WIDTH - 1:CONV_WIDTH, :]
    for t in range(CONV_WIDTH - 1):
        back = CONV_WIDTH - 1 - t
        y = y + xbuf_ref[halo - back:halo - back + tl, :] * cw_ref[t:t + 1, :]
    xbuf_ref[0:halo, :] = x[tl - halo:, :]

    gates = jnp.dot(y.astype(_BF16), wg_ref[...], preferred_element_type=_F32)
    r = jax.nn.sigmoid(gates[:, :W] + ba_ref[...])
    ig = jax.nn.sigmoid(gates[:, W:] + bx_ref[...])
    log_a = -LRU_C * r * jax.nn.softplus(-lam_ref[...])
    a = jnp.exp(log_a)
    one_minus_a2 = -jnp.tanh(log_a) * (1.0 + a * a)
    b = jnp.sqrt(jnp.maximum(one_minus_a2, 1e-12)) * (ig * y)

    s = 1
    while s < tl:
        a_sh = _shift_rows(a, s, 1.0)
        b_sh = _shift_rows(b, s, 0.0)
        b = a * b_sh + b
        a = a * a_sh
        s *= 2
    h = b + a * hprev_ref[0:1, :]
    hprev_ref[...] = jnp.broadcast_to(h[tl - 1:tl, :], hprev_ref.shape)
    o_ref[...] = (h * jax.nn.gelu(gate_ref[...])).astype(o_ref.dtype)


def _rglru(xr, gate, conv_w, conv_b, wg_bf16, b_a, b_x, lru_lambda, tl):
    T, W = xr.shape
    row = pl.BlockSpec((tl, W), lambda i: (i, 0))
    vec = pl.BlockSpec((1, W), lambda i: (0, 0))
    return pl.pallas_call(
        functools.partial(_rglru_kernel, tl=tl),
        grid=(T // tl,),
        in_specs=[row, row,
                  pl.BlockSpec(conv_w.shape, lambda i: (0, 0)),
                  vec,
                  pl.BlockSpec(wg_bf16.shape, lambda i: (0, 0)),
                  vec, vec, vec],
        out_specs=row,
        out_shape=jax.ShapeDtypeStruct((T, W), _BF16),
        scratch_shapes=[pltpu.VMEM((tl + SUBLANES, W), _F32),
                        pltpu.VMEM((SUBLANES, W), _F32)],
        compiler_params=_cparams("arbitrary"),
        name="rglru",
    )(xr, gate, conv_w, conv_b, wg_bf16, b_a, b_x, lru_lambda)


def _oddeven_mergesort_pairs(n):
    pairs = []
    p = 1
    while p < n:
        k = p
        while k >= 1:
            for j in range(k % p, n - k, 2 * k):
                for i in range(min(k, n - j - k)):
                    if (i + j) // (2 * p) == (i + j + k) // (2 * p):
                        pairs.append((i + j, i + j + k))
            k //= 2
        p *= 2
    return pairs


_SORT16 = _oddeven_mergesort_pairs(PEER_TOPK)


def _sort_desc(xs):
    xs = list(xs)
    for lo, hi in _SORT16:
        a, b = xs[lo], xs[hi]
        xs[lo], xs[hi] = jnp.maximum(a, b), jnp.minimum(a, b)
    return xs


def _top_of_union(a_sorted, b_sorted):
    n = len(a_sorted)
    return [jnp.maximum(a_sorted[i], b_sorted[n - 1 - i]) for i in range(n)]


def _bitonic_merge_desc(xs):
    xs = list(xs)
    n = len(xs)
    d = n // 2
    while d >= 1:
        for i in range(n):
            if i & d == 0:
                a, b = xs[i], xs[i + d]
                xs[i], xs[i + d] = jnp.maximum(a, b), jnp.minimum(a, b)
        d //= 2
    return xs


def _prefix_count(pred, vals):
    assert len(vals) == PEER_TOPK == 16
    bits = []
    step = 8
    while step >= 1:
        cands = [vals[base + step - 1] for base in range(0, 16, 2 * step)]
        for b in reversed(bits):
            cands = [jnp.where(b, hi, lo) for lo, hi in zip(cands[0::2], cands[1::2])]
        bits.append(pred(cands[0]))
        step //= 2
    count = jnp.where(pred(vals[15]), 1.0, 0.0)
    for b, w in zip(bits, (8.0, 4.0, 2.0, 1.0)):
        count = count + jnp.where(b, w, 0.0)
    return count


def _pair_threshold(t1, t2):
    k = PEER_TOPK
    rows = [[t1[a] + t2[b] for b in range(k // (a + 1))] for a in range(k)]
    neg = jnp.full_like(t1[0], NEG_INF)
    g1 = _sort_desc(rows[1] + rows[2] + rows[4])
    g2 = _sort_desc(rows[3] + rows[5] + rows[6] + rows[7]
                    + [rows[a][0] for a in range(8, 14)])
    g3 = _sort_desc([rows[14][0], rows[15][0]] + [neg] * (k - 2))
    top = _bitonic_merge_desc(_top_of_union(rows[0], g1))
    top = _bitonic_merge_desc(_top_of_union(top, g2))
    top = _top_of_union(top, g3)
    return top, rows[0][0]


def _mid_kernel(attn_ref, lru_ref, x_ref, wo_ref, g2_ref, wqt_ref, k1_ref, k2_ref,
                h_ref, xn_ref, c1_ref, e1_ref, r2_ref, e2_ref, qt_ref, sorted_ref, top2_ref, stat_ref):
    aw = attn_ref.shape[-1]
    h = (x_ref[...]
         + jnp.dot(attn_ref[...], wo_ref[0:aw, :], preferred_element_type=_F32)
         + jnp.dot(lru_ref[...], wo_ref[aw:, :], preferred_element_type=_F32))
    h_ref[...] = h
    xn = _rms(h, g2_ref[...]).astype(_BF16)
    xn_ref[...] = xn
    qt_ref[...] = lax.dot_general(wqt_ref[...], xn, _NT_DIMS, preferred_element_type=_F32)

    nslab = PEER_N_KEYS // SUBLANES
    tm = qt_ref.shape[-1]
    assert PEER_HEADS == SUBLANES, "phase 2 puts one head on each sublane"

    def scores(hd):
        base = pl.multiple_of(hd * (2 * PEER_HALF), 2 * PEER_HALF)
        q1 = qt_ref[pl.ds(base, PEER_HALF), :].astype(_BF16)
        q2 = qt_ref[pl.ds(base + PEER_HALF, PEER_HALF), :].astype(_BF16)
        return (jnp.dot(k1_ref[...], q1, preferred_element_type=_F32),
                jnp.dot(k2_ref[...], q2, preferred_element_type=_F32))

    def sort_head(hd, carry):
        for side, s in enumerate(scores(hd)):
            srt = _sort_desc([s[j * SUBLANES:(j + 1) * SUBLANES, :] for j in range(nslab)])
            for b in range(PEER_TOPK):
                for sl in range(SUBLANES):
                    sorted_ref[side, b, sl, pl.ds(hd, 1), :] = srt[b][sl:sl + 1, :]
        return carry

    lax.fori_loop(0, PEER_HEADS, sort_head, 0)

    def merged_top16(side):
        lists = [[sorted_ref[side, b, sl] for b in range(PEER_TOPK)] for sl in range(SUBLANES)]
        while len(lists) > 1:
            lists = [_bitonic_merge_desc(_top_of_union(a, b)) for a, b in zip(lists[0::2], lists[1::2])]
        return lists[0]

    t1, t2 = merged_top16(0), merged_top16(1)
    top, m = _pair_threshold(t1, t2)
    z = functools.reduce(jnp.add, [jnp.exp(t - m) for t in top])
    for b in range(PEER_TOPK):
        top2_ref[b] = t2[b]
    stat_ref[0] = functools.reduce(jnp.minimum, top)
    stat_ref[1] = t1[0]
    stat_ref[2] = 1.0 / z

    def tables(hd, carry):
        def head_row(ref, k):
            return jnp.broadcast_to(ref[k, pl.ds(hd, 1), :], (SUBLANES, tm))

        s1, s2 = scores(hd)
        t2h = [head_row(top2_ref, b) for b in range(PEER_TOPK)]
        tau, m1, inv_z = head_row(stat_ref, 0), head_row(stat_ref, 1), head_row(stat_ref, 2)
        ranks, e2s = [], []
        for j in range(nslab):
            rows = slice(j * SUBLANES, (j + 1) * SUBLANES)
            s1j, s2j = s1[rows, :], s2[rows, :]
            cnt = _prefix_count(lambda v: s1j + v >= tau, t2h)
            rank = _prefix_count(lambda v: v > s2j, t2h)
            c1_ref[hd, rows, :] = cnt
            e1_ref[hd, rows, :] = jnp.exp(s1j - m1)
            ranks.append(rank)
            e2s.append(jnp.exp(s2j - t2h[0]) * inv_z)
        r2_ref[hd] = jnp.concatenate(ranks, axis=0).astype(_BF16)
        e2_ref[hd] = jnp.concatenate(e2s, axis=0).astype(_BF16)
        return carry

    lax.fori_loop(0, PEER_HEADS, tables, 0)


def _mid(attn, lru, x2, wo_bf16, g2, wqt_bf16, k1_bf16, k2_bf16, tm):
    T, D = x2.shape
    aw, lw = attn.shape[1], lru.shape[1]
    tab_f32 = jax.ShapeDtypeStruct((PEER_HEADS, PEER_N_KEYS, T), _F32)
    tab_bf16 = jax.ShapeDtypeStruct((PEER_HEADS, PEER_N_KEYS, T), _BF16)
    tab_spec = pl.BlockSpec((PEER_HEADS, PEER_N_KEYS, tm), lambda i: (0, 0, i))
    const = lambda a: pl.BlockSpec(a.shape, lambda i: (0,) * a.ndim)
    return pl.pallas_call(
        _mid_kernel,
        grid=(T // tm,),
        in_specs=[pl.BlockSpec((tm, aw), lambda i: (i, 0)),
                  pl.BlockSpec((tm, lw), lambda i: (i, 0)),
                  pl.BlockSpec((tm, D), lambda i: (i, 0)),
                  const(wo_bf16), const(g2), const(wqt_bf16), const(k1_bf16), const(k2_bf16)],
        out_specs=[pl.BlockSpec((tm, D), lambda i: (i, 0)),
                   pl.BlockSpec((tm, D), lambda i: (i, 0)),
                   tab_spec, tab_spec, tab_spec, tab_spec],
        out_shape=[jax.ShapeDtypeStruct((T, D), _F32),
                   jax.ShapeDtypeStruct((T, D), _BF16),
                   tab_f32, tab_f32, tab_bf16, tab_bf16],
        scratch_shapes=[pltpu.VMEM((wqt_bf16.shape[0], tm), _F32),
                        pltpu.VMEM((2, PEER_TOPK, SUBLANES, PEER_HEADS, tm), _F32),
                        pltpu.VMEM((PEER_TOPK, PEER_HEADS, tm), _F32),
                        pltpu.VMEM((3, PEER_HEADS, tm), _F32)],
        compiler_params=_cparams("parallel"),
        name="mid_proj_topk",
    )(attn, lru, x2, wo_bf16, g2, wqt_bf16, k1_bf16, k2_bf16)


def _gelu_tanh(x):
    c1 = 2.0 * math.sqrt(2.0 / math.pi)
    c2 = c1 * 0.044715
    return x / (1.0 + jnp.exp(x * (-c1 - c2 * (x * x))))


def _peer_kernel(xn_ref, down_ref, upt_ref, c1_ref, e1_ref, r2_ref, e2_ref, h_ref, gf_ref,
                 o_ref, acc_ref, hid_ref, *, te):
    j = pl.program_id(1)
    nk = PEER_N_KEYS
    tm = xn_ref.shape[0]
    W = BF16_VREG_LANES
    R = 64
    nb = R // SUBLANES
    mb = 2 * nk

    @pl.when(j == 0)
    def _():
        acc_ref[...] = jnp.zeros(acc_ref.shape, _F32)

    row0 = pl.multiple_of(j * (te // nk), SUBLANES)
    zero = jnp.zeros((nb, SUBLANES, W), _BF16)
    xn = xn_ref[...]
    for m in range(te // mb):
        act = lax.dot_general(down_ref[m * mb:(m + 1) * mb, :], xn, _NT_DIMS,
                              preferred_element_type=_F32)
        for c in range(tm // W):
            lanes = slice(c * W, (c + 1) * W)
            for a in range(mb // nk):
                i1 = m * (mb // nk) + a
                cnts, e1s = [], []
                for hd in range(PEER_HEADS):
                    grp = pl.ds(row0 + (i1 // SUBLANES) * SUBLANES, SUBLANES)
                    c1_row = c1_ref[hd, grp, lanes][i1 % SUBLANES:i1 % SUBLANES + 1, :]
                    e1_row = e1_ref[hd, grp, lanes][i1 % SUBLANES:i1 % SUBLANES + 1, :]
                    cnts.append(jnp.broadcast_to(c1_row, (SUBLANES, W)).astype(_BF16)[None])
                    e1s.append(jnp.broadcast_to(e1_row, (SUBLANES, W)).astype(_BF16)[None])
                for rb in range(nk // R):
                    gate = zero
                    for hd in range(PEER_HEADS):
                        r2 = r2_ref[hd, rb * R:(rb + 1) * R, lanes].reshape(nb, SUBLANES, W)
                        e2 = e2_ref[hd, rb * R:(rb + 1) * R, lanes].reshape(nb, SUBLANES, W)
                        gate = gate + jnp.where(r2 < cnts[hd], e2, zero) * e1s[hd]
                    g = _gelu_tanh(act[a * nk + rb * R:a * nk + (rb + 1) * R, lanes].astype(_BF16))
                    rows = slice(i1 * nk + rb * R, i1 * nk + (rb + 1) * R)
                    hid_ref[rows, lanes] = (g.reshape(nb, SUBLANES, W) * gate).reshape(R, W)
    acc_ref[...] += jnp.dot(upt_ref[...], hid_ref[...], preferred_element_type=_F32)

    @pl.when(j == pl.num_programs(1) - 1)
    def _():
        o_ref[...] = _rms(h_ref[...] + jnp.transpose(acc_ref[...]), gf_ref[...])


def _peer(xn, down_bf16, upt_bf16, c1, e1, r2, e2, h, gf, tm, te):
    T, D = xn.shape
    E = down_bf16.shape[0]
    nk = PEER_N_KEYS
    assert te % (nk * SUBLANES) == 0, "an expert tile spans whole sublane groups of first-key rows"
    tab_spec = pl.BlockSpec((PEER_HEADS, nk, tm), lambda i, j: (0, 0, i))
    return pl.pallas_call(
        functools.partial(_peer_kernel, te=te),
        grid=(T // tm, E // te),
        in_specs=[pl.BlockSpec((tm, D), lambda i, j: (i, 0)),
                  pl.BlockSpec((te, D), lambda i, j: (j, 0)),
                  pl.BlockSpec((D, te), lambda i, j: (0, j)),
                  tab_spec, tab_spec, tab_spec, tab_spec,
                  pl.BlockSpec((tm, D), lambda i, j: (i, 0)),
                  pl.BlockSpec((1, D), lambda i, j: (0, 0))],
        out_specs=pl.BlockSpec((tm, D), lambda i, j: (i, 0)),
        out_shape=jax.ShapeDtypeStruct((T, D), _F32),
        scratch_shapes=[pltpu.VMEM((D, tm), _F32), pltpu.VMEM((te, tm), _BF16)],
        compiler_params=_cparams("parallel", "arbitrary"),
        name="peer_experts",
    )(xn, down_bf16, upt_bf16, c1, e1, r2, e2, h, gf)


def _block_diag(w):
    n, d, _ = w.shape
    eye = jnp.eye(n, dtype=w.dtype)
    return (eye[:, None, :, None] * w[:, :, None, :]).reshape(n * d, n * d)


def _tile(n, want):
    t = min(n, want)
    assert n % t == 0, (n, want)
    return t


def kernel(x, norm1_g, w_in, lambda_q1, lambda_k1, lambda_q2, lambda_k2, subln_g, conv_w, conv_b,
           w_rec_gate, b_rec_gate, w_in_gate, b_in_gate, lru_lambda, w_out, norm2_g, w_query,
           sub_keys_1, sub_keys_2, expert_down, expert_up, norm_f_g):
    B, S, D = x.shape
    T = B * S
    assert B == 1, "sequence mixing kernels assume a single sequence"
    assert w_in.shape[0] == 1, "the final norm is fused into the (single) layer's PEER kernel"
    l = 0
    h = x.reshape(T, D)
    row = lambda v: v.reshape(1, -1)
    lambda_init = 0.8 - 0.6 * math.exp(-0.3 * l)
    q, k, v, xr, gate = _in_proj(h, row(norm1_g[l]), w_in[l].astype(_BF16), _tile(T, 512))
    attn = _attention(q, k, v, row(lambda_q1[l]), row(lambda_k1[l]), row(lambda_q2[l]),
                      row(lambda_k2[l]), row(subln_g[l]), lambda_init, _tile(T, 512), 4)
    w_gates = jnp.concatenate([_block_diag(w_rec_gate[l]), _block_diag(w_in_gate[l])],
                              axis=1).astype(_BF16)
    lru = _rglru(xr, gate, conv_w[l], row(conv_b[l]), w_gates, row(b_rec_gate[l]),
                 row(b_in_gate[l]), row(lru_lambda[l]), _tile(T, 256))
    h_mid, xnt, c1, e1, r2, e2 = _mid(
        attn, lru, h, w_out[l].astype(_BF16), row(norm2_g[l]),
        jnp.transpose(w_query[l]).astype(_BF16), sub_keys_1[l].astype(_BF16),
        sub_keys_2[l].astype(_BF16), _tile(T, 512))
    out = _peer(xnt, expert_down[l].astype(_BF16), jnp.transpose(expert_up[l]).astype(_BF16),
                c1, e1, r2, e2, h_mid, row(norm_f_g), _tile(T, 512), 2048)
    return out.reshape(B, S, D)
```

```python
import functools
import math

import jax
import jax.numpy as jnp
from jax import lax
from jax.experimental import pallas as pl
from jax.experimental.pallas import tpu as pltpu

CHUNK = 64
ATTN_HEAD_DIM = 64
ATTN_HEADS = 4
ATTN_WIDTH = 2 * ATTN_HEAD_DIM * ATTN_HEADS
LRU_BLOCKS = 8
CONV_WIDTH = 4
LRU_C = 8.0
PEER_HEADS = 8
PEER_N_KEYS = 128
PEER_HALF = 128
PEER_TOPK = 16
EPS = 1e-6
NEG_INF = -1e30

LANES = 128
SUBLANES = 8
BF16_VREG_LANES = 256
VMEM_LIMIT_BYTES = 48 * 1024 * 1024

_F32 = jnp.float32
_BF16 = jnp.bfloat16
_NT_DIMS = (((1,), (1,)), ((), ()))


def _cparams(*sem, flags=None):
    return pltpu.CompilerParams(dimension_semantics=sem, vmem_limit_bytes=VMEM_LIMIT_BYTES, flags=flags)


def _rms(x, g):
    return x * lax.rsqrt(jnp.mean(x * x, axis=-1, keepdims=True) + EPS) * g


def _in_proj_kernel(x_ref, g_ref, w_ref, cw_ref, cb_ref, wg_ref, ba_ref, bx_ref, lam_ref,
                    q_ref, k_ref, v_ref, lru_ref, xbuf_ref, hprev_ref):
    n = _rms(x_ref[...], g_ref[...]).astype(_BF16)
    hw = 2 * ATTN_HEAD_DIM

    def proj(c0, width):
        return jnp.dot(n, w_ref[:, c0:c0 + width], preferred_element_type=_F32)

    scale = ATTN_HEAD_DIM ** -0.5 * math.log2(math.e)
    ones_col = (lax.broadcasted_iota(jnp.int32, (x_ref.shape[0], hw), 1) == 0).astype(_BF16)
    q_all = proj(0, ATTN_WIDTH) * scale
    k_all = proj(ATTN_WIDTH, ATTN_WIDTH)
    v_all = proj(2 * ATTN_WIDTH, ATTN_WIDTH)
    for h in range(ATTN_HEADS):
        cols = slice(h * hw, (h + 1) * hw)
        q_ref[h] = q_all[:, cols].astype(_BF16)
        k_ref[h] = k_all[:, cols].astype(_BF16)
        v_ref[h, :, 0:hw] = v_all[:, cols].astype(_BF16)
        v_ref[h, :, hw:2 * hw] = ones_col
    lru_w = lru_ref.shape[-1]
    xr = proj(3 * ATTN_WIDTH, lru_w)
    gate = proj(3 * ATTN_WIDTH + lru_w, lru_w)
    lru_ref[...] = _rglru_tile(xr, gate, cw_ref, cb_ref, wg_ref, ba_ref, bx_ref, lam_ref,
                               xbuf_ref, hprev_ref).astype(lru_ref.dtype)


def _in_proj(x2, g, w_bf16, conv_w, conv_b, wg_bf16, b_a, b_x, lru_lambda, tm):
    T, D = x2.shape
    lru_w = (w_bf16.shape[1] - 3 * ATTN_WIDTH) // 2
    hw = 2 * ATTN_HEAD_DIM
    head_shape = jax.ShapeDtypeStruct((ATTN_HEADS, T, hw), _BF16)
    head_spec = pl.BlockSpec((ATTN_HEADS, tm, hw), lambda i: (0, i, 0))
    v_shape = jax.ShapeDtypeStruct((ATTN_HEADS, T, 2 * hw), _BF16)
    v_spec = pl.BlockSpec((ATTN_HEADS, tm, 2 * hw), lambda i: (0, i, 0))
    const = lambda a: pl.BlockSpec(a.shape, lambda i: (0,) * a.ndim)
    return pl.pallas_call(
        _in_proj_kernel,
        grid=(T // tm,),
        in_specs=[pl.BlockSpec((tm, D), lambda i: (i, 0)),
                  const(g), const(w_bf16), const(conv_w), const(conv_b), const(wg_bf16),
                  const(b_a), const(b_x), const(lru_lambda)],
        out_specs=[head_spec, head_spec, v_spec, pl.BlockSpec((tm, lru_w), lambda i: (i, 0))],
        out_shape=[head_shape, head_shape, v_shape, jax.ShapeDtypeStruct((T, lru_w), _BF16)],
        scratch_shapes=[pltpu.VMEM((tm + SUBLANES, lru_w), _F32),
                        pltpu.VMEM((SUBLANES, lru_w), _F32)],
        compiler_params=_cparams("arbitrary"),
        name="in_proj_rglru",
    )(x2, g, w_bf16, conv_w, conv_b, wg_bf16, b_a, b_x, lru_lambda)


def _attn_kernel(lq1_ref, lk1_ref, lq2_ref, lk2_ref, sg_ref, q_ref, k_ref, v_ref, o_ref,
                 m_ref, acc_ref, *, blk, nsub, lambda_init):
    i = pl.program_id(1)
    hd = ATTN_HEAD_DIM
    hw = 2 * hd
    lane = lax.broadcasted_iota(jnp.int32, (blk, hw), 1)
    q_maps = []
    for sb in range(nsub):
        q = q_ref[sb * blk:(sb + 1) * blk, :]
        zero = jnp.zeros_like(q)
        q_maps.append((jnp.where(lane < hd, q, zero), jnp.where(lane >= hd, q, zero)))

    m_ref[...] = jnp.full(m_ref.shape, NEG_INF, _F32)
    acc_ref[...] = jnp.zeros(acc_ref.shape, _F32)

    def chain_update(sb, mp, row0, width, ends_on_diagonal):
        kj = k_ref[pl.ds(row0, width), :]
        vj = v_ref[pl.ds(row0, width), :]
        s = lax.dot_general(q_maps[sb][mp], kj, _NT_DIMS, preferred_element_type=_F32)
        if ends_on_diagonal:
            qc = lax.broadcasted_iota(jnp.int32, (blk, width), 0) // CHUNK + (width - blk) // CHUNK
            kc = lax.broadcasted_iota(jnp.int32, (blk, width), 1) // CHUNK
            s = jnp.where(qc >= kc, s, NEG_INF)
        idx = 2 * sb + mp
        m_prev = m_ref[idx]
        m_next = jnp.maximum(m_prev, jnp.max(s, axis=1, keepdims=True))
        alpha = jnp.exp2(m_prev - m_next)
        e = jnp.exp2(s - jnp.tile(m_next, (1, width // LANES)))
        m_ref[idx] = m_next
        pv = jnp.dot(e.astype(_BF16), vj, preferred_element_type=_F32)
        acc_ref[idx] = jnp.tile(alpha, (1, 2 * hw // LANES)) * acc_ref[idx] + pv

    bq = nsub * blk
    kw = 2 * blk

    def body(j, carry):
        for u in range(nsub):
            row0 = pl.multiple_of(j * bq + u * blk, blk)
            for sb in range(nsub):
                for mp in range(2):
                    chain_update(sb, mp, row0, blk, False)
        return carry

    lax.fori_loop(0, i, body, 0)

    base = pl.multiple_of(i * bq, bq)
    for sb in range(nsub):
        done, visible = 0, (sb + 1) * blk
        while done < visible:
            width = min(kw, visible - done)
            for mp in range(2):
                chain_update(sb, mp, base + done, width, done + width == visible)
            done += width

    lam = (jnp.exp(jnp.sum(lq1_ref[...] * lk1_ref[...], keepdims=True))
           - jnp.exp(jnp.sum(lq2_ref[...] * lk2_ref[...], keepdims=True)) + lambda_init)
    for sb in range(nsub):
        a0, a1 = acc_ref[2 * sb], acc_ref[2 * sb + 1]
        o = a0[:, 0:hw] / a0[:, hw:hw + 1] - lam * (a1[:, 0:hw] / a1[:, hw:hw + 1])
        o = o * lax.rsqrt(jnp.mean(o * o, axis=-1, keepdims=True) + EPS)
        o_ref[sb * blk:(sb + 1) * blk, :] = (o * sg_ref[...] * (1.0 - lambda_init)).astype(o_ref.dtype)


def _attention(q, k, v, lq1, lk1, lq2, lk2, subln_g, lambda_init, blk, nsub):
    H, T, hw = q.shape
    bq = blk * nsub
    vec = pl.BlockSpec((1, ATTN_HEAD_DIM), lambda h, i: (0, 0))
    kernel = functools.partial(_attn_kernel, blk=blk, nsub=nsub, lambda_init=lambda_init)
    return pl.pallas_call(
        kernel,
        grid=(H, T // bq),
        in_specs=[vec, vec, vec, vec,
                  pl.BlockSpec((1, hw), lambda h, i: (0, 0)),
                  pl.BlockSpec((None, bq, hw), lambda h, i: (h, i, 0)),
                  pl.BlockSpec((None, T, hw), lambda h, i: (h, 0, 0), pipeline_mode=pl.Buffered(1)),
                  pl.BlockSpec((None, T, 2 * hw), lambda h, i: (h, 0, 0), pipeline_mode=pl.Buffered(1))],
        out_specs=pl.BlockSpec((bq, hw), lambda h, i: (i, h)),
        out_shape=jax.ShapeDtypeStruct((T, H * hw), _BF16),
        scratch_shapes=[pltpu.VMEM((2 * nsub, blk, LANES), _F32),
                        pltpu.VMEM((2 * nsub, blk, 2 * hw), _F32)],
        compiler_params=_cparams("parallel", "parallel"),
        name="diff_attention",
    )(lq1, lk1, lq2, lk2, subln_g, q, k, v)


def _shift_rows(x, s, fill):
    rolled = pltpu.roll(x, s, axis=0)
    row = lax.broadcasted_iota(jnp.int32, x.shape, 0)
    return jnp.where(row >= s, rolled, fill)


def _rglru_tile(x, gate, cw_ref, cb_ref, wg_ref, ba_ref, bx_ref, lam_ref, xbuf_ref, hprev_ref):
    halo = SUBLANES
    tl, W = x.shape

    @pl.when(pl.program_id(0) == 0)
    def _():
        xbuf_ref[0:halo, :] = jnp.zeros((halo, W), _F32)
        hprev_ref[...] = jnp.zeros(hprev_ref.shape, _F32)

    xbuf_ref[halo:halo + tl, :] = x
    y = cb_ref[...] + x * cw_ref[CONV_WIDTH - 1:CONV_WIDTH, :]
    for t in range(CONV_WIDTH - 1):
        back = CONV_WIDTH - 1 - t
        y = y + xbuf_ref[halo - back:halo - back + tl, :] * cw_ref[t:t + 1, :]
    xbuf_ref[0:halo, :] = x[tl - halo:, :]

    gates = jnp.dot(y.astype(_BF16), wg_ref[...], preferred_element_type=_F32)
    r = jax.nn.sigmoid(gates[:, :W] + ba_ref[...])
    ig = jax.nn.sigmoid(gates[:, W:] + bx_ref[...])
    log_a = -LRU_C * r * jax.nn.softplus(-lam_ref[...])
    a = jnp.exp(log_a)
    one_minus_a2 = -jnp.tanh(log_a) * (1.0 + a * a)
    b = jnp.sqrt(jnp.maximum(one_minus_a2, 1e-12)) * (ig * y)

    s = 1
    while s < tl:
        a_sh = _shift_rows(a, s, 1.0)
        b_sh = _shift_rows(b, s, 0.0)
        b = a * b_sh + b
        a = a * a_sh
        s *= 2
    h = b + a * hprev_ref[0:1, :]
    hprev_ref[...] = jnp.broadcast_to(h[tl - 1:tl, :], hprev_ref.shape)
    return h * jax.nn.gelu(gate)


def _oddeven_mergesort_pairs(n):
    pairs = []
    p = 1
    while p < n:
        k = p
        while k >= 1:
            for j in range(k % p, n - k, 2 * k):
                for i in range(min(k, n - j - k)):
                    if (i + j) // (2 * p) == (i + j + k) // (2 * p):
                        pairs.append((i + j, i + j + k))
            k //= 2
        p *= 2
    return pairs


_SORT16 = _oddeven_mergesort_pairs(PEER_TOPK)


def _sort_desc(xs):
    xs = list(xs)
    for lo, hi in _SORT16:
        a, b = xs[lo], xs[hi]
        xs[lo], xs[hi] = jnp.maximum(a, b), jnp.minimum(a, b)
    return xs


def _top_of_union(a_sorted, b_sorted):
    n = len(a_sorted)
    return [jnp.maximum(a_sorted[i], b_sorted[n - 1 - i]) for i in range(n)]


def _bitonic_merge_desc(xs):
    xs = list(xs)
    n = len(xs)
    d = n // 2
    while d >= 1:
        for i in range(n):
            if i & d == 0:
                a, b = xs[i], xs[i + d]
                xs[i], xs[i + d] = jnp.maximum(a, b), jnp.minimum(a, b)
        d //= 2
    return xs


def _prefix_count(pred, vals):
    assert len(vals) == PEER_TOPK == 16
    bits = []
    step = 8
    while step >= 1:
        cands = [vals[base + step - 1] for base in range(0, 16, 2 * step)]
        for b in reversed(bits):
            cands = [jnp.where(b, hi, lo) for lo, hi in zip(cands[0::2], cands[1::2])]
        bits.append(pred(cands[0]))
        step //= 2
    count = jnp.where(pred(vals[15]), 1.0, 0.0)
    for b, w in zip(bits, (8.0, 4.0, 2.0, 1.0)):
        count = count + jnp.where(b, w, 0.0)
    return count


def _pair_threshold(t1, t2):
    k = PEER_TOPK
    rows = [[t1[a] + t2[b] for b in range(k // (a + 1))] for a in range(k)]
    neg = jnp.full_like(t1[0], NEG_INF)
    g1 = _sort_desc(rows[1] + rows[2] + rows[4])
    g2 = _sort_desc(rows[3] + rows[5] + rows[6] + rows[7]
                    + [rows[a][0] for a in range(8, 14)])
    g3 = _sort_desc([rows[14][0], rows[15][0]] + [neg] * (k - 2))
    top = _bitonic_merge_desc(_top_of_union(rows[0], g1))
    top = _bitonic_merge_desc(_top_of_union(top, g2))
    top = _top_of_union(top, g3)
    return top, rows[0][0]


def _mid_kernel(attn_ref, lru_ref, x_ref, wo_ref, g2_ref, wqt_ref, k1_ref, k2_ref,
                h_ref, xn_ref, c1_ref, e1_ref, r2_ref, e2_ref, qt_ref, sorted_ref, top2_ref, stat_ref):
    aw = attn_ref.shape[-1]
    h = (x_ref[...]
         + jnp.dot(attn_ref[...], wo_ref[0:aw, :], preferred_element_type=_F32)
         + jnp.dot(lru_ref[...], wo_ref[aw:, :], preferred_element_type=_F32))
    h_ref[...] = h
    xn = _rms(h, g2_ref[...]).astype(_BF16)
    xn_ref[...] = xn

    nslab = PEER_N_KEYS // SUBLANES
    tm = qt_ref.shape[-1]
    qw = 2 * PEER_HALF
    assert PEER_HEADS == SUBLANES, "phase 2 puts one head on each sublane"

    def scores(qt_head):
        q1 = qt_head[0:PEER_HALF, :].astype(_BF16)
        q2 = qt_head[PEER_HALF:qw, :].astype(_BF16)
        return (jnp.dot(k1_ref[...], q1, preferred_element_type=_F32),
                jnp.dot(k2_ref[...], q2, preferred_element_type=_F32))

    for hd in range(PEER_HEADS):
        qt_head = lax.dot_general(wqt_ref[hd * qw:(hd + 1) * qw, :], xn, _NT_DIMS,
                                  preferred_element_type=_F32)
        qt_ref[hd * qw:(hd + 1) * qw, :] = qt_head
        for side, s in enumerate(scores(qt_head)):
            srt = _sort_desc([s[j * SUBLANES:(j + 1) * SUBLANES, :] for j in range(nslab)])
            for b in range(PEER_TOPK):
                for sl in range(SUBLANES):
                    sorted_ref[side, b, sl, hd:hd + 1, :] = srt[b][sl:sl + 1, :]

    def merged_top16(side):
        lists = [[sorted_ref[side, b, sl] for b in range(PEER_TOPK)] for sl in range(SUBLANES)]
        while len(lists) > 1:
            lists = [_bitonic_merge_desc(_top_of_union(a, b)) for a, b in zip(lists[0::2], lists[1::2])]
        return lists[0]

    t1, t2 = merged_top16(0), merged_top16(1)
    top, m = _pair_threshold(t1, t2)
    z = functools.reduce(jnp.add, [jnp.exp(t - m) for t in top])
    for b in range(PEER_TOPK):
        top2_ref[b] = t2[b]
    stat_ref[0] = functools.reduce(jnp.minimum, top)
    stat_ref[1] = t1[0]
    stat_ref[2] = 1.0 / z

    def tables(hd, carry):
        def head_row(ref, k):
            return jnp.broadcast_to(ref[k, pl.ds(hd, 1), :], (SUBLANES, tm))

        s1, s2 = scores(qt_ref[pl.ds(pl.multiple_of(hd * qw, qw), qw), :])
        t2h = [head_row(top2_ref, b) for b in range(PEER_TOPK)]
        tau, m1, inv_z = head_row(stat_ref, 0), head_row(stat_ref, 1), head_row(stat_ref, 2)
        ranks, e2s = [], []
        for j in range(nslab):
            rows = slice(j * SUBLANES, (j + 1) * SUBLANES)
            s1j, s2j = s1[rows, :], s2[rows, :]
            cnt = _prefix_count(lambda v: s1j + v >= tau, t2h)
            rank = _prefix_count(lambda v: v > s2j, t2h)
            c1_ref[hd, rows, :] = cnt
            e1_ref[hd, rows, :] = jnp.exp(s1j - m1)
            ranks.append(rank)
            e2s.append(jnp.exp(s2j - t2h[0]) * inv_z)
        r2_ref[hd] = jnp.concatenate(ranks, axis=0).astype(_BF16)
        e2_ref[hd] = jnp.concatenate(e2s, axis=0).astype(_BF16)
        return carry

    lax.fori_loop(0, PEER_HEADS, tables, 0)


def _mid(attn, lru, x2, wo_bf16, g2, wqt_bf16, k1_bf16, k2_bf16, tm):
    T, D = x2.shape
    aw, lw = attn.shape[1], lru.shape[1]
    tab_f32 = jax.ShapeDtypeStruct((PEER_HEADS, PEER_N_KEYS, T), _F32)
    tab_bf16 = jax.ShapeDtypeStruct((PEER_HEADS, PEER_N_KEYS, T), _BF16)
    tab_spec = pl.BlockSpec((PEER_HEADS, PEER_N_KEYS, tm), lambda i: (0, 0, i))
    const = lambda a: pl.BlockSpec(a.shape, lambda i: (0,) * a.ndim)
    return pl.pallas_call(
        _mid_kernel,
        grid=(T // tm,),
        in_specs=[pl.BlockSpec((tm, aw), lambda i: (i, 0)),
                  pl.BlockSpec((tm, lw), lambda i: (i, 0)),
                  pl.BlockSpec((tm, D), lambda i: (i, 0)),
                  const(wo_bf16), const(g2), const(wqt_bf16), const(k1_bf16), const(k2_bf16)],
        out_specs=[pl.BlockSpec((tm, D), lambda i: (i, 0)),
                   pl.BlockSpec((tm, D), lambda i: (i, 0)),
                   tab_spec, tab_spec, tab_spec, tab_spec],
        out_shape=[jax.ShapeDtypeStruct((T, D), _F32),
                   jax.ShapeDtypeStruct((T, D), _BF16),
                   tab_f32, tab_f32, tab_bf16, tab_bf16],
        scratch_shapes=[pltpu.VMEM((wqt_bf16.shape[0], tm), _F32),
                        pltpu.VMEM((2, PEER_TOPK, SUBLANES, PEER_HEADS, tm), _F32),
                        pltpu.VMEM((PEER_TOPK, PEER_HEADS, tm), _F32),
                        pltpu.VMEM((3, PEER_HEADS, tm), _F32)],
        compiler_params=_cparams("parallel"),
        name="mid_proj_topk",
    )(attn, lru, x2, wo_bf16, g2, wqt_bf16, k1_bf16, k2_bf16)


def _gelu_tanh(x):
    c1 = 2.0 * math.sqrt(2.0 / math.pi)
    c2 = c1 * 0.044715
    return x / (1.0 + jnp.exp(x * (-c1 - c2 * (x * x))))


def _peer_kernel(xn_ref, down_ref, upt_ref, c1_ref, e1_ref, r2_ref, e2_ref, h_ref, gf_ref,
                 o_ref, acc_ref, hid_ref, *, te):
    j = pl.program_id(1)
    nk = PEER_N_KEYS
    tm = xn_ref.shape[0]
    W = BF16_VREG_LANES
    R = 64
    nb = R // SUBLANES
    mb = 2 * nk

    @pl.when(j == 0)
    def _():
        acc_ref[...] = jnp.zeros(acc_ref.shape, _F32)

    row0 = pl.multiple_of(j * (te // nk), SUBLANES)
    zero = jnp.zeros((nb, SUBLANES, W), _BF16)
    xn = xn_ref[...]
    for m in range(te // mb):
        act = lax.dot_general(down_ref[m * mb:(m + 1) * mb, :], xn, _NT_DIMS,
                              preferred_element_type=_F32)
        for c in range(tm // W):
            lanes = slice(c * W, (c + 1) * W)
            for a in range(mb // nk):
                i1 = m * (mb // nk) + a
                cnts, e1s = [], []
                for hd in range(PEER_HEADS):
                    grp = pl.ds(row0 + (i1 // SUBLANES) * SUBLANES, SUBLANES)
                    c1_row = c1_ref[hd, grp, lanes][i1 % SUBLANES:i1 % SUBLANES + 1, :]
                    e1_row = e1_ref[hd, grp, lanes][i1 % SUBLANES:i1 % SUBLANES + 1, :]
                    cnts.append(jnp.broadcast_to(c1_row, (SUBLANES, W)).astype(_BF16)[None])
                    e1s.append(jnp.broadcast_to(e1_row, (SUBLANES, W)).astype(_BF16)[None])
                for rb in range(nk // R):
                    gate = zero
                    for hd in range(PEER_HEADS):
                        r2 = r2_ref[hd, rb * R:(rb + 1) * R, lanes].reshape(nb, SUBLANES, W)
                        e2 = e2_ref[hd, rb * R:(rb + 1) * R, lanes].reshape(nb, SUBLANES, W)
                        gate = gate + jnp.where(r2 < cnts[hd], e2, zero) * e1s[hd]
                    g = _gelu_tanh(act[a * nk + rb * R:a * nk + (rb + 1) * R, lanes].astype(_BF16))
                    rows = slice(i1 * nk + rb * R, i1 * nk + (rb + 1) * R)
                    hid_ref[rows, lanes] = (g.reshape(nb, SUBLANES, W) * gate).reshape(R, W)
    acc_ref[...] += jnp.dot(upt_ref[...], hid_ref[...], preferred_element_type=_F32)

    @pl.when(j == pl.num_programs(1) - 1)
    def _():
        o_ref[...] = _rms(h_ref[...] + jnp.transpose(acc_ref[...]), gf_ref[...])


def _peer(xn, down_bf16, upt_bf16, c1, e1, r2, e2, h, gf, tm, te):
    T, D = xn.shape
    E = down_bf16.shape[0]
    nk = PEER_N_KEYS
    assert te % (nk * SUBLANES) == 0, "an expert tile spans whole sublane groups of first-key rows"
    tab_spec = pl.BlockSpec((PEER_HEADS, nk, tm), lambda i, j: (0, 0, i))
    return pl.pallas_call(
        functools.partial(_peer_kernel, te=te),
        grid=(T // tm, E // te),
        in_specs=[pl.BlockSpec((tm, D), lambda i, j: (i, 0)),
                  pl.BlockSpec((te, D), lambda i, j: (j, 0)),
                  pl.BlockSpec((D, te), lambda i, j: (0, j)),
                  tab_spec, tab_spec, tab_spec, tab_spec,
                  pl.BlockSpec((tm, D), lambda i, j: (i, 0)),
                  pl.BlockSpec((1, D), lambda i, j: (0, 0))],
        out_specs=pl.BlockSpec((tm, D), lambda i, j: (i, 0)),
        out_shape=jax.ShapeDtypeStruct((T, D), _F32),
        scratch_shapes=[pltpu.VMEM((D, tm), _F32), pltpu.VMEM((te, tm), _BF16)],
        compiler_params=_cparams("parallel", "arbitrary"),
        name="peer_experts",
    )(xn, down_bf16, upt_bf16, c1, e1, r2, e2, h, gf)


def _block_diag(w):
    n, d, _ = w.shape
    eye = jnp.eye(n, dtype=w.dtype)
    return (eye[:, None, :, None] * w[:, :, None, :]).reshape(n * d, n * d)


def _tile(n, want):
    t = min(n, want)
    assert n % t == 0, (n, want)
    return t


def kernel(x, norm1_g, w_in, lambda_q1, lambda_k1, lambda_q2, lambda_k2, subln_g, conv_w, conv_b,
           w_rec_gate, b_rec_gate, w_in_gate, b_in_gate, lru_lambda, w_out, norm2_g, w_query,
           sub_keys_1, sub_keys_2, expert_down, expert_up, norm_f_g):
    B, S, D = x.shape
    T = B * S
    assert B == 1, "sequence mixing kernels assume a single sequence"
    assert w_in.shape[0] == 1, "the final norm is fused into the (single) layer's PEER kernel"
    l = 0
    h = x.reshape(T, D)
    row = lambda v: v.reshape(1, -1)
    lambda_init = 0.8 - 0.6 * math.exp(-0.3 * l)
    w_gates = jnp.concatenate([_block_diag(w_rec_gate[l]), _block_diag(w_in_gate[l])],
                              axis=1).astype(_BF16)
    q, k, v, lru = _in_proj(h, row(norm1_g[l]), w_in[l].astype(_BF16), conv_w[l], row(conv_b[l]),
                            w_gates, row(b_rec_gate[l]), row(b_in_gate[l]), row(lru_lambda[l]),
                            _tile(T, 512))
    attn = _attention(q, k, v, row(lambda_q1[l]), row(lambda_k1[l]), row(lambda_q2[l]),
                      row(lambda_k2[l]), row(subln_g[l]), lambda_init, _tile(T, 512), 4)
    h_mid, xnt, c1, e1, r2, e2 = _mid(
        attn, lru, h, w_out[l].astype(_BF16), row(norm2_g[l]),
        jnp.transpose(w_query[l]).astype(_BF16), sub_keys_1[l].astype(_BF16),
        sub_keys_2[l].astype(_BF16), _tile(T, 512))
    out = _peer(xnt, expert_down[l].astype(_BF16), jnp.transpose(expert_up[l]).astype(_BF16),
                c1, e1, r2, e2, h_mid, row(norm_f_g), _tile(T, 512), 2048)
    return out.reshape(B, S, D)
```

```python
import functools
import math

import jax
import jax.numpy as jnp
from jax import lax
from jax.experimental import pallas as pl
from jax.experimental.pallas import tpu as pltpu

CHUNK = 64
ATTN_HEAD_DIM = 64
ATTN_HEADS = 4
ATTN_WIDTH = 2 * ATTN_HEAD_DIM * ATTN_HEADS
LRU_BLOCKS = 8
CONV_WIDTH = 4
LRU_C = 8.0
PEER_HEADS = 8
PEER_N_KEYS = 128
PEER_HALF = 128
PEER_TOPK = 16
EPS = 1e-6
NEG_INF = -1e30

LANES = 128
SUBLANES = 8
BF16_VREG_LANES = 256
VMEM_LIMIT_BYTES = 48 * 1024 * 1024

_F32 = jnp.float32
_BF16 = jnp.bfloat16
_NT_DIMS = (((1,), (1,)), ((), ()))


def _cparams(*sem, flags=None):
    return pltpu.CompilerParams(dimension_semantics=sem, vmem_limit_bytes=VMEM_LIMIT_BYTES, flags=flags)


def _rms(x, g):
    return x * lax.rsqrt(jnp.mean(x * x, axis=-1, keepdims=True) + EPS) * g


def _in_proj_kernel(x_ref, g_ref, w_ref, cw_ref, cb_ref, wg_ref, ba_ref, bx_ref, lam_ref,
                    q_ref, k_ref, v_ref, lru_ref, xbuf_ref, hprev_ref):
    n = _rms(x_ref[...], g_ref[...]).astype(_BF16)
    hw = 2 * ATTN_HEAD_DIM

    def proj(c0, width):
        return jnp.dot(n, w_ref[:, c0:c0 + width], preferred_element_type=_F32)

    scale = ATTN_HEAD_DIM ** -0.5 * math.log2(math.e)
    ones_col = (lax.broadcasted_iota(jnp.int32, (x_ref.shape[0], hw), 1) == 0).astype(_BF16)
    q_all = proj(0, ATTN_WIDTH) * scale
    k_all = proj(ATTN_WIDTH, ATTN_WIDTH)
    v_all = proj(2 * ATTN_WIDTH, ATTN_WIDTH)
    for h in range(ATTN_HEADS):
        cols = slice(h * hw, (h + 1) * hw)
        q_ref[h] = q_all[:, cols].astype(_BF16)
        k_ref[h] = k_all[:, cols].astype(_BF16)
        v_ref[h, :, 0:hw] = v_all[:, cols].astype(_BF16)
        v_ref[h, :, hw:2 * hw] = ones_col
    lru_w = lru_ref.shape[-1]
    xr = proj(3 * ATTN_WIDTH, lru_w)
    gate = proj(3 * ATTN_WIDTH + lru_w, lru_w)
    lru_ref[...] = _rglru_tile(xr, gate, cw_ref, cb_ref, wg_ref, ba_ref, bx_ref, lam_ref,
                               xbuf_ref, hprev_ref).astype(lru_ref.dtype)


def _in_proj(x2, g, w_bf16, conv_w, conv_b, wg_bf16, b_a, b_x, lru_lambda, tm):
    T, D = x2.shape
    lru_w = (w_bf16.shape[1] - 3 * ATTN_WIDTH) // 2
    hw = 2 * ATTN_HEAD_DIM
    head_shape = jax.ShapeDtypeStruct((ATTN_HEADS, T, hw), _BF16)
    head_spec = pl.BlockSpec((ATTN_HEADS, tm, hw), lambda i: (0, i, 0))
    v_shape = jax.ShapeDtypeStruct((ATTN_HEADS, T, 2 * hw), _BF16)
    v_spec = pl.BlockSpec((ATTN_HEADS, tm, 2 * hw), lambda i: (0, i, 0))
    const = lambda a: pl.BlockSpec(a.shape, lambda i: (0,) * a.ndim)
    return pl.pallas_call(
        _in_proj_kernel,
        grid=(T // tm,),
        in_specs=[pl.BlockSpec((tm, D), lambda i: (i, 0)),
                  const(g), const(w_bf16), const(conv_w), const(conv_b), const(wg_bf16),
                  const(b_a), const(b_x), const(lru_lambda)],
        out_specs=[head_spec, head_spec, v_spec, pl.BlockSpec((tm, lru_w), lambda i: (i, 0))],
        out_shape=[head_shape, head_shape, v_shape, jax.ShapeDtypeStruct((T, lru_w), _BF16)],
        scratch_shapes=[pltpu.VMEM((tm + SUBLANES, lru_w), _F32),
                        pltpu.VMEM((SUBLANES, lru_w), _F32)],
        compiler_params=_cparams("arbitrary"),
        name="in_proj_rglru",
    )(x2, g, w_bf16, conv_w, conv_b, wg_bf16, b_a, b_x, lru_lambda)


def _attn_kernel(lq1_ref, lk1_ref, lq2_ref, lk2_ref, sg_ref, q_ref, k_ref, v_ref, o_ref,
                 m_ref, acc_ref, *, blk, nsub, lambda_init):
    i = pl.program_id(1)
    hd = ATTN_HEAD_DIM
    hw = 2 * hd
    lane = lax.broadcasted_iota(jnp.int32, (blk, hw), 1)
    q_maps = []
    for sb in range(nsub):
        q = q_ref[sb * blk:(sb + 1) * blk, :]
        zero = jnp.zeros_like(q)
        q_maps.append((jnp.where(lane < hd, q, zero), jnp.where(lane >= hd, q, zero)))

    m_ref[...] = jnp.full(m_ref.shape, NEG_INF, _F32)
    acc_ref[...] = jnp.zeros(acc_ref.shape, _F32)

    def chain_update(sb, mp, row0, width, ends_on_diagonal):
        kj = k_ref[pl.ds(row0, width), :]
        vj = v_ref[pl.ds(row0, width), :]
        s = lax.dot_general(q_maps[sb][mp], kj, _NT_DIMS, preferred_element_type=_F32)
        if ends_on_diagonal:
            qc = lax.broadcasted_iota(jnp.int32, (blk, width), 0) // CHUNK + (width - blk) // CHUNK
            kc = lax.broadcasted_iota(jnp.int32, (blk, width), 1) // CHUNK
            s = jnp.where(qc >= kc, s, NEG_INF)
        idx = 2 * sb + mp
        m_prev = m_ref[idx]
        m_next = jnp.maximum(m_prev, jnp.max(s, axis=1, keepdims=True))
        alpha = jnp.exp2(m_prev - m_next)
        e = jnp.exp2(s - jnp.tile(m_next, (1, width // LANES)))
        m_ref[idx] = m_next
        pv = jnp.dot(e.astype(_BF16), vj, preferred_element_type=_F32)
        acc_ref[idx] = jnp.tile(alpha, (1, 2 * hw // LANES)) * acc_ref[idx] + pv

    bq = nsub * blk
    kw = 2 * blk

    def body(j, carry):
        for u in range(nsub):
            row0 = pl.multiple_of(j * bq + u * blk, blk)
            for sb in range(nsub):
                for mp in range(2):
                    chain_update(sb, mp, row0, blk, False)
        return carry

    lax.fori_loop(0, i, body, 0)

    base = pl.multiple_of(i * bq, bq)
    for sb in range(nsub):
        done, visible = 0, (sb + 1) * blk
        while done < visible:
            width = min(kw, visible - done)
            for mp in range(2):
                chain_update(sb, mp, base + done, width, done + width == visible)
            done += width

    lam = (jnp.exp(jnp.sum(lq1_ref[...] * lk1_ref[...], keepdims=True))
           - jnp.exp(jnp.sum(lq2_ref[...] * lk2_ref[...], keepdims=True)) + lambda_init)
    for sb in range(nsub):
        a0, a1 = acc_ref[2 * sb], acc_ref[2 * sb + 1]
        o = a0[:, 0:hw] / a0[:, hw:hw + 1] - lam * (a1[:, 0:hw] / a1[:, hw:hw + 1])
        o = o * lax.rsqrt(jnp.mean(o * o, axis=-1, keepdims=True) + EPS)
        o_ref[sb * blk:(sb + 1) * blk, :] = (o * sg_ref[...] * (1.0 - lambda_init)).astype(o_ref.dtype)


def _attention(q, k, v, lq1, lk1, lq2, lk2, subln_g, lambda_init, blk, nsub):
    H, T, hw = q.shape
    bq = blk * nsub
    vec = pl.BlockSpec((1, ATTN_HEAD_DIM), lambda h, i: (0, 0))
    kernel = functools.partial(_attn_kernel, blk=blk, nsub=nsub, lambda_init=lambda_init)
    return pl.pallas_call(
        kernel,
        grid=(H, T // bq),
        in_specs=[vec, vec, vec, vec,
                  pl.BlockSpec((1, hw), lambda h, i: (0, 0)),
                  pl.BlockSpec((None, bq, hw), lambda h, i: (h, i, 0)),
                  pl.BlockSpec((None, T, hw), lambda h, i: (h, 0, 0), pipeline_mode=pl.Buffered(1)),
                  pl.BlockSpec((None, T, 2 * hw), lambda h, i: (h, 0, 0), pipeline_mode=pl.Buffered(1))],
        out_specs=pl.BlockSpec((bq, hw), lambda h, i: (i, h)),
        out_shape=jax.ShapeDtypeStruct((T, H * hw), _BF16),
        scratch_shapes=[pltpu.VMEM((2 * nsub, blk, LANES), _F32),
                        pltpu.VMEM((2 * nsub, blk, 2 * hw), _F32)],
        compiler_params=_cparams("parallel", "parallel"),
        name="diff_attention",
    )(lq1, lk1, lq2, lk2, subln_g, q, k, v)


def _shift_rows(x, s, fill):
    rolled = pltpu.roll(x, s, axis=0)
    row = lax.broadcasted_iota(jnp.int32, x.shape, 0)
    return jnp.where(row >= s, rolled, fill)


def _rglru_tile(x, gate, cw_ref, cb_ref, wg_ref, ba_ref, bx_ref, lam_ref, xbuf_ref, hprev_ref):
    halo = SUBLANES
    tl, W = x.shape

    @pl.when(pl.program_id(0) == 0)
    def _():
        xbuf_ref[0:halo, :] = jnp.zeros((halo, W), _F32)
        hprev_ref[...] = jnp.zeros(hprev_ref.shape, _F32)

    xbuf_ref[halo:halo + tl, :] = x
    y = cb_ref[...] + x * cw_ref[CONV_WIDTH - 1:CONV_WIDTH, :]
    for t in range(CONV_WIDTH - 1):
        back = CONV_WIDTH - 1 - t
        y = y + xbuf_ref[halo - back:halo - back + tl, :] * cw_ref[t:t + 1, :]
    xbuf_ref[0:halo, :] = x[tl - halo:, :]

    gates = jnp.dot(y.astype(_BF16), wg_ref[...], preferred_element_type=_F32)
    r = jax.nn.sigmoid(gates[:, :W] + ba_ref[...])
    ig = jax.nn.sigmoid(gates[:, W:] + bx_ref[...])
    log_a = -LRU_C * r * jax.nn.softplus(-lam_ref[...])
    a = jnp.exp(log_a)
    one_minus_a2 = -jnp.tanh(log_a) * (1.0 + a * a)
    b = jnp.sqrt(jnp.maximum(one_minus_a2, 1e-12)) * (ig * y)

    s = 1
    while s < tl:
        a_sh = _shift_rows(a, s, 1.0)
        b_sh = _shift_rows(b, s, 0.0)
        b = a * b_sh + b
        a = a * a_sh
        s *= 2
    h = b + a * hprev_ref[0:1, :]
    hprev_ref[...] = jnp.broadcast_to(h[tl - 1:tl, :], hprev_ref.shape)
    return h * jax.nn.gelu(gate)


def _oddeven_mergesort_pairs(n):
    pairs = []
    p = 1
    while p < n:
        k = p
        while k >= 1:
            for j in range(k % p, n - k, 2 * k):
                for i in range(min(k, n - j - k)):
                    if (i + j) // (2 * p) == (i + j + k) // (2 * p):
                        pairs.append((i + j, i + j + k))
            k //= 2
        p *= 2
    return pairs


_SORT16 = _oddeven_mergesort_pairs(PEER_TOPK)


def _sort_desc(xs):
    xs = list(xs)
    for lo, hi in _SORT16:
        a, b = xs[lo], xs[hi]
        xs[lo], xs[hi] = jnp.maximum(a, b), jnp.minimum(a, b)
    return xs


def _top_of_union(a_sorted, b_sorted):
    n = len(a_sorted)
    return [jnp.maximum(a_sorted[i], b_sorted[n - 1 - i]) for i in range(n)]


def _bitonic_merge_desc(xs):
    xs = list(xs)
    n = len(xs)
    d = n // 2
    while d >= 1:
        for i in range(n):
            if i & d == 0:
                a, b = xs[i], xs[i + d]
                xs[i], xs[i + d] = jnp.maximum(a, b), jnp.minimum(a, b)
        d //= 2
    return xs


def _prefix_count(pred, vals):
    assert len(vals) == PEER_TOPK == 16
    bits = []
    step = 8
    while step >= 1:
        cands = [vals[base + step - 1] for base in range(0, 16, 2 * step)]
        for b in reversed(bits):
            cands = [jnp.where(b, hi, lo) for lo, hi in zip(cands[0::2], cands[1::2])]
        bits.append(pred(cands[0]))
        step //= 2
    count = jnp.where(pred(vals[15]), 1.0, 0.0)
    for b, w in zip(bits, (8.0, 4.0, 2.0, 1.0)):
        count = count + jnp.where(b, w, 0.0)
    return count


def _pair_threshold(t1, t2):
    k = PEER_TOPK
    rows = [[t1[a] + t2[b] for b in range(k // (a + 1))] for a in range(k)]
    neg = jnp.full_like(t1[0], NEG_INF)
    g1 = _sort_desc(rows[1] + rows[2] + rows[4])
    g2 = _sort_desc(rows[3] + rows[5] + rows[6] + rows[7]
                    + [rows[a][0] for a in range(8, 14)])
    g3 = _sort_desc([rows[14][0], rows[15][0]] + [neg] * (k - 2))
    top = _bitonic_merge_desc(_top_of_union(rows[0], g1))
    top = _bitonic_merge_desc(_top_of_union(top, g2))
    top = _top_of_union(top, g3)
    return top, rows[0][0]


def _mid_kernel(attn_ref, lru_ref, x_ref, wo_ref, g2_ref, wqt_ref, k1_ref, k2_ref,
                h_ref, xn_ref, c1_ref, e1_ref, r2_ref, e2_ref, qt_ref, sorted_ref, top2_ref, stat_ref):
    aw = attn_ref.shape[-1]
    h = (x_ref[...]
         + jnp.dot(attn_ref[...], wo_ref[0:aw, :], preferred_element_type=_F32)
         + jnp.dot(lru_ref[...], wo_ref[aw:, :], preferred_element_type=_F32))
    h_ref[...] = h
    xn = _rms(h, g2_ref[...]).astype(_BF16)
    xn_ref[...] = xn

    nslab = PEER_N_KEYS // SUBLANES
    tm = qt_ref.shape[-1]
    qw = 2 * PEER_HALF
    assert PEER_HEADS == SUBLANES, "phase 2 puts one head on each sublane"

    def scores(qt_head):
        q1 = qt_head[0:PEER_HALF, :].astype(_BF16)
        q2 = qt_head[PEER_HALF:qw, :].astype(_BF16)
        return (jnp.dot(k1_ref[...], q1, preferred_element_type=_F32),
                jnp.dot(k2_ref[...], q2, preferred_element_type=_F32))

    qt_ref[...] = lax.dot_general(wqt_ref[...], xn, _NT_DIMS, preferred_element_type=_F32)

    def head_queries(hd):
        return qt_ref[pl.ds(pl.multiple_of(hd * qw, qw), qw), :]

    def sort_head(hd, carry):
        for side, s in enumerate(scores(head_queries(hd))):
            srt = _sort_desc([s[j * SUBLANES:(j + 1) * SUBLANES, :] for j in range(nslab)])
            for b in range(PEER_TOPK):
                for sl in range(SUBLANES):
                    sorted_ref[side, b, sl, pl.ds(hd, 1), :] = srt[b][sl:sl + 1, :]
        return carry

    lax.fori_loop(0, PEER_HEADS, sort_head, 0)

    def merged_top16(side):
        lists = [[sorted_ref[side, b, sl] for b in range(PEER_TOPK)] for sl in range(SUBLANES)]
        while len(lists) > 1:
            lists = [_bitonic_merge_desc(_top_of_union(a, b)) for a, b in zip(lists[0::2], lists[1::2])]
        return lists[0]

    t1, t2 = merged_top16(0), merged_top16(1)
    top, m = _pair_threshold(t1, t2)
    z = functools.reduce(jnp.add, [jnp.exp(t - m) for t in top])
    for b in range(PEER_TOPK):
        top2_ref[b] = t2[b]
    stat_ref[0] = functools.reduce(jnp.minimum, top)
    stat_ref[1] = t1[0]
    stat_ref[2] = 1.0 / z

    def tables(hd, carry):
        def head_row(ref, k):
            return jnp.broadcast_to(ref[k, pl.ds(hd, 1), :], (SUBLANES, tm))

        s1, s2 = scores(head_queries(hd))
        t2h = [head_row(top2_ref, b) for b in range(PEER_TOPK)]
        tau, m1, inv_z = head_row(stat_ref, 0), head_row(stat_ref, 1), head_row(stat_ref, 2)
        ranks, e2s = [], []
        for j in range(nslab):
            rows = slice(j * SUBLANES, (j + 1) * SUBLANES)
            s1j, s2j = s1[rows, :], s2[rows, :]
            cnt = _prefix_count(lambda v: s1j + v >= tau, t2h)
            rank = _prefix_count(lambda v: v > s2j, t2h)
            c1_ref[hd, rows, :] = cnt
            e1_ref[hd, rows, :] = jnp.exp(s1j - m1)
            ranks.append(rank)
            e2s.append(jnp.exp(s2j - t2h[0]) * inv_z)
        r2_ref[hd] = jnp.concatenate(ranks, axis=0).astype(_BF16)
        e2_ref[hd] = jnp.concatenate(e2s, axis=0).astype(_BF16)
        return carry

    lax.fori_loop(0, PEER_HEADS, tables, 0)


def _mid(attn, lru, x2, wo_bf16, g2, wqt_bf16, k1_bf16, k2_bf16, tm):
    T, D = x2.shape
    aw, lw = attn.shape[1], lru.shape[1]
    tab_f32 = jax.ShapeDtypeStruct((PEER_HEADS, PEER_N_KEYS, T), _F32)
    tab_bf16 = jax.ShapeDtypeStruct((PEER_HEADS, PEER_N_KEYS, T), _BF16)
    tab_spec = pl.BlockSpec((PEER_HEADS, PEER_N_KEYS, tm), lambda i: (0, 0, i))
    const = lambda a: pl.BlockSpec(a.shape, lambda i: (0,) * a.ndim)
    return pl.pallas_call(
        _mid_kernel,
        grid=(T // tm,),
        in_specs=[pl.BlockSpec((tm, aw), lambda i: (i, 0)),
                  pl.BlockSpec((tm, lw), lambda i: (i, 0)),
                  pl.BlockSpec((tm, D), lambda i: (i, 0)),
                  const(wo_bf16), const(g2), const(wqt_bf16), const(k1_bf16), const(k2_bf16)],
        out_specs=[pl.BlockSpec((tm, D), lambda i: (i, 0)),
                   pl.BlockSpec((tm, D), lambda i: (i, 0)),
                   tab_spec, tab_spec, tab_spec, tab_spec],
        out_shape=[jax.ShapeDtypeStruct((T, D), _F32),
                   jax.ShapeDtypeStruct((T, D), _BF16),
                   tab_f32, tab_f32, tab_bf16, tab_bf16],
        scratch_shapes=[pltpu.VMEM((wqt_bf16.shape[0], tm), _F32),
                        pltpu.VMEM((2, PEER_TOPK, SUBLANES, PEER_HEADS, tm), _F32),
                        pltpu.VMEM((PEER_TOPK, PEER_HEADS, tm), _F32),
                        pltpu.VMEM((3, PEER_HEADS, tm), _F32)],
        compiler_params=_cparams("parallel"),
        name="mid_proj_topk",
    )(attn, lru, x2, wo_bf16, g2, wqt_bf16, k1_bf16, k2_bf16)


def _gelu_tanh(x):
    c1 = 2.0 * math.sqrt(2.0 / math.pi)
    c2 = c1 * 0.044715
    return x / (1.0 + jnp.exp(x * (-c1 - c2 * (x * x))))


def _peer_kernel(xn_ref, down_ref, upt_ref, c1_ref, e1_ref, r2_ref, e2_ref, h_ref, gf_ref,
                 o_ref, acc_ref, hid_ref, *, te):
    j = pl.program_id(1)
    nk = PEER_N_KEYS
    tm = xn_ref.shape[0]
    W = BF16_VREG_LANES
    R = 64
    nb = R // SUBLANES
    mb = 2 * nk

    @pl.when(j == 0)
    def _():
        acc_ref[...] = jnp.zeros(acc_ref.shape, _F32)

    row0 = pl.multiple_of(j * (te // nk), SUBLANES)
    zero = jnp.zeros((nb, SUBLANES, W), _BF16)
    xn = xn_ref[...]
    for m in range(te // mb):
        act = lax.dot_general(down_ref[m * mb:(m + 1) * mb, :], xn, _NT_DIMS,
                              preferred_element_type=_F32)
        for c in range(tm // W):
            lanes = slice(c * W, (c + 1) * W)
            for a in range(mb // nk):
                i1 = m * (mb // nk) + a
                cnts, e1s = [], []
                for hd in range(PEER_HEADS):
                    grp = pl.ds(row0 + (i1 // SUBLANES) * SUBLANES, SUBLANES)
                    c1_row = c1_ref[hd, grp, lanes][i1 % SUBLANES:i1 % SUBLANES + 1, :]
                    e1_row = e1_ref[hd, grp, lanes][i1 % SUBLANES:i1 % SUBLANES + 1, :]
                    cnts.append(jnp.broadcast_to(c1_row, (SUBLANES, W)).astype(_BF16)[None])
                    e1s.append(jnp.broadcast_to(e1_row, (SUBLANES, W)).astype(_BF16)[None])
                for rb in range(nk // R):
                    gate = zero
                    for hd in range(PEER_HEADS):
                        r2 = r2_ref[hd, rb * R:(rb + 1) * R, lanes].reshape(nb, SUBLANES, W)
                        e2 = e2_ref[hd, rb * R:(rb + 1) * R, lanes].reshape(nb, SUBLANES, W)
                        gate = gate + jnp.where(r2 < cnts[hd], e2, zero) * e1s[hd]
                    g = _gelu_tanh(act[a * nk + rb * R:a * nk + (rb + 1) * R, lanes].astype(_BF16))
                    rows = slice(i1 * nk + rb * R, i1 * nk + (rb + 1) * R)
                    hid_ref[rows, lanes] = (g.reshape(nb, SUBLANES, W) * gate).reshape(R, W)
    acc_ref[...] += jnp.dot(upt_ref[...], hid_ref[...], preferred_element_type=_F32)

    @pl.when(j == pl.num_programs(1) - 1)
    def _():
        o_ref[...] = _rms(h_ref[...] + jnp.transpose(acc_ref[...]), gf_ref[...])


def _peer(xn, down_bf16, upt_bf16, c1, e1, r2, e2, h, gf, tm, te):
    T, D = xn.shape
    E = down_bf16.shape[0]
    nk = PEER_N_KEYS
    assert te % (nk * SUBLANES) == 0, "an expert tile spans whole sublane groups of first-key rows"
    tab_spec = pl.BlockSpec((PEER_HEADS, nk, tm), lambda i, j: (0, 0, i))
    return pl.pallas_call(
        functools.partial(_peer_kernel, te=te),
        grid=(T // tm, E // te),
        in_specs=[pl.BlockSpec((tm, D), lambda i, j: (i, 0)),
                  pl.BlockSpec((te, D), lambda i, j: (j, 0)),
                  pl.BlockSpec((D, te), lambda i, j: (0, j)),
                  tab_spec, tab_spec, tab_spec, tab_spec,
                  pl.BlockSpec((tm, D), lambda i, j: (i, 0)),
                  pl.BlockSpec((1, D), lambda i, j: (0, 0))],
        out_specs=pl.BlockSpec((tm, D), lambda i, j: (i, 0)),
        out_shape=jax.ShapeDtypeStruct((T, D), _F32),
        scratch_shapes=[pltpu.VMEM((D, tm), _F32), pltpu.VMEM((te, tm), _BF16)],
        compiler_params=_cparams("parallel", "arbitrary"),
        name="peer_experts",
    )(xn, down_bf16, upt_bf16, c1, e1, r2, e2, h, gf)


def _block_diag(w):
    n, d, _ = w.shape
    eye = jnp.eye(n, dtype=w.dtype)
    return (eye[:, None, :, None] * w[:, :, None, :]).reshape(n * d, n * d)


def _tile(n, want):
    t = min(n, want)
    assert n % t == 0, (n, want)
    return t


def kernel(x, norm1_g, w_in, lambda_q1, lambda_k1, lambda_q2, lambda_k2, subln_g, conv_w, conv_b,
           w_rec_gate, b_rec_gate, w_in_gate, b_in_gate, lru_lambda, w_out, norm2_g, w_query,
           sub_keys_1, sub_keys_2, expert_down, expert_up, norm_f_g):
    B, S, D = x.shape
    T = B * S
    assert B == 1, "sequence mixing kernels assume a single sequence"
    assert w_in.shape[0] == 1, "the final norm is fused into the (single) layer's PEER kernel"
    l = 0
    h = x.reshape(T, D)
    row = lambda v: v.reshape(1, -1)
    lambda_init = 0.8 - 0.6 * math.exp(-0.3 * l)
    w_gates = jnp.concatenate([_block_diag(w_rec_gate[l]), _block_diag(w_in_gate[l])],
                              axis=1).astype(_BF16)
    q, k, v, lru = _in_proj(h, row(norm1_g[l]), w_in[l].astype(_BF16), conv_w[l], row(conv_b[l]),
                            w_gates, row(b_rec_gate[l]), row(b_in_gate[l]), row(lru_lambda[l]),
                            _tile(T, 512))
    attn = _attention(q, k, v, row(lambda_q1[l]), row(lambda_k1[l]), row(lambda_q2[l]),
                      row(lambda_k2[l]), row(subln_g[l]), lambda_init, _tile(T, 512), 4)
    h_mid, xnt, c1, e1, r2, e2 = _mid(
        attn, lru, h, w_out[l].astype(_BF16), row(norm2_g[l]),
        jnp.transpose(w_query[l]).astype(_BF16), sub_keys_1[l].astype(_BF16),
        sub_keys_2[l].astype(_BF16), _tile(T, 512))
    out = _peer(xnt, expert_down[l].astype(_BF16), jnp.transpose(expert_up[l]).astype(_BF16),
                c1, e1, r2, e2, h_mid, row(norm_f_g), _tile(T, 512), 2048)
    return out.reshape(B, S, D)
```

```python
import functools
import math

import jax
import jax.numpy as jnp
from jax import lax
from jax.experimental import pallas as pl
from jax.experimental.pallas import tpu as pltpu

CHUNK = 64
ATTN_HEAD_DIM = 64
ATTN_HEADS = 4
ATTN_WIDTH = 2 * ATTN_HEAD_DIM * ATTN_HEADS
LRU_BLOCKS = 8
CONV_WIDTH = 4
LRU_C = 8.0
PEER_HEADS = 8
PEER_N_KEYS = 128
PEER_HALF = 128
PEER_TOPK = 16
EPS = 1e-6
NEG_INF = -1e30

LANES = 128
SUBLANES = 8
BF16_VREG_LANES = 256
VMEM_LIMIT_BYTES = 48 * 1024 * 1024

_F32 = jnp.float32
_BF16 = jnp.bfloat16
_NT_DIMS = (((1,), (1,)), ((), ()))


def _cparams(*sem):
    return pltpu.CompilerParams(dimension_semantics=sem, vmem_limit_bytes=VMEM_LIMIT_BYTES)


def _rms(x, g):
    return x * lax.rsqrt(jnp.mean(x * x, axis=-1, keepdims=True) + EPS) * g


def _in_proj_kernel(x_ref, g_ref, w_ref, cw_ref, cb_ref, wg_ref, ba_ref, bx_ref, lam_ref,
                    q_ref, k_ref, v_ref, lru_ref, xbuf_ref, hprev_ref):
    n = _rms(x_ref[...], g_ref[...]).astype(_BF16)
    hw = 2 * ATTN_HEAD_DIM

    def proj(c0, width):
        return jnp.dot(n, w_ref[:, c0:c0 + width], preferred_element_type=_F32)

    scale = ATTN_HEAD_DIM ** -0.5 * math.log2(math.e)
    ones_col = (lax.broadcasted_iota(jnp.int32, (x_ref.shape[0], hw), 1) == 0).astype(_BF16)
    q_all = proj(0, ATTN_WIDTH) * scale
    k_all = proj(ATTN_WIDTH, ATTN_WIDTH)
    v_all = proj(2 * ATTN_WIDTH, ATTN_WIDTH)
    for h in range(ATTN_HEADS):
        cols = slice(h * hw, (h + 1) * hw)
        q_ref[h] = q_all[:, cols].astype(_BF16)
        k_ref[h] = k_all[:, cols].astype(_BF16)
        v_ref[h, :, 0:hw] = v_all[:, cols].astype(_BF16)
        v_ref[h, :, hw:2 * hw] = ones_col
    lru_w = lru_ref.shape[-1]
    xr = proj(3 * ATTN_WIDTH, lru_w)
    gate = proj(3 * ATTN_WIDTH + lru_w, lru_w)
    @pl.when(pl.program_id(0) == 0)
    def _():
        xbuf_ref[0:SUBLANES, :] = jnp.zeros((SUBLANES, lru_w), _F32)
        hprev_ref[...] = jnp.zeros(hprev_ref.shape, _F32)

    tl = xbuf_ref.shape[0] - SUBLANES
    for r0 in range(0, x_ref.shape[0], tl):
        out = _rglru_tile(xr[r0:r0 + tl], gate[r0:r0 + tl], cw_ref, cb_ref, wg_ref, ba_ref, bx_ref,
                          lam_ref, xbuf_ref, hprev_ref)
        lru_ref[r0:r0 + tl, :] = out.astype(lru_ref.dtype)


def _in_proj(x2, g, w_bf16, conv_w, conv_b, wg_bf16, b_a, b_x, lru_lambda, tm):
    T, D = x2.shape
    lru_w = (w_bf16.shape[1] - 3 * ATTN_WIDTH) // 2
    hw = 2 * ATTN_HEAD_DIM
    head_shape = jax.ShapeDtypeStruct((ATTN_HEADS, T, hw), _BF16)
    head_spec = pl.BlockSpec((ATTN_HEADS, tm, hw), lambda i: (0, i, 0))
    v_shape = jax.ShapeDtypeStruct((ATTN_HEADS, T, 2 * hw), _BF16)
    v_spec = pl.BlockSpec((ATTN_HEADS, tm, 2 * hw), lambda i: (0, i, 0))
    const = lambda a: pl.BlockSpec(a.shape, lambda i: (0,) * a.ndim)
    return pl.pallas_call(
        _in_proj_kernel,
        grid=(T // tm,),
        in_specs=[pl.BlockSpec((tm, D), lambda i: (i, 0)),
                  const(g), const(w_bf16), const(conv_w), const(conv_b), const(wg_bf16),
                  const(b_a), const(b_x), const(lru_lambda)],
        out_specs=[head_spec, head_spec, v_spec, pl.BlockSpec((tm, lru_w), lambda i: (i, 0))],
        out_shape=[head_shape, head_shape, v_shape, jax.ShapeDtypeStruct((T, lru_w), _BF16)],
        scratch_shapes=[pltpu.VMEM((tm // 2 + SUBLANES, lru_w), _F32),
                        pltpu.VMEM((SUBLANES, lru_w), _F32)],
        compiler_params=_cparams("arbitrary"),
        name="in_proj_rglru",
    )(x2, g, w_bf16, conv_w, conv_b, wg_bf16, b_a, b_x, lru_lambda)


def _attn_kernel(lq1_ref, lk1_ref, lq2_ref, lk2_ref, sg_ref, q_ref, k_ref, v_ref, o_ref,
                 m_ref, acc_ref, *, blk, nsub, lambda_init):
    i = pl.program_id(1)
    hd = ATTN_HEAD_DIM
    hw = 2 * hd
    lane = lax.broadcasted_iota(jnp.int32, (blk, hw), 1)
    q_maps = []
    for sb in range(nsub):
        q = q_ref[sb * blk:(sb + 1) * blk, :]
        zero = jnp.zeros_like(q)
        q_maps.append((jnp.where(lane < hd, q, zero), jnp.where(lane >= hd, q, zero)))

    m_ref[...] = jnp.full(m_ref.shape, NEG_INF, _F32)
    acc_ref[...] = jnp.zeros(acc_ref.shape, _F32)

    def chain_update(sb, mp, row0, width, ends_on_diagonal):
        kj = k_ref[pl.ds(row0, width), :]
        vj = v_ref[pl.ds(row0, width), :]
        s = lax.dot_general(q_maps[sb][mp], kj, _NT_DIMS, preferred_element_type=_F32)
        if ends_on_diagonal:
            qc = lax.broadcasted_iota(jnp.int32, (blk, width), 0) // CHUNK + (width - blk) // CHUNK
            kc = lax.broadcasted_iota(jnp.int32, (blk, width), 1) // CHUNK
            s = jnp.where(qc >= kc, s, NEG_INF)
        idx = 2 * sb + mp
        m_prev = m_ref[idx]
        m_next = jnp.maximum(m_prev, jnp.max(s, axis=1, keepdims=True))
        alpha = jnp.exp2(m_prev - m_next)
        e = jnp.exp2(s - jnp.tile(m_next, (1, width // LANES)))
        m_ref[idx] = m_next
        pv = jnp.dot(e.astype(_BF16), vj, preferred_element_type=_F32)
        acc_ref[idx] = jnp.tile(alpha, (1, 2 * hw // LANES)) * acc_ref[idx] + pv

    bq = nsub * blk
    kw = 2 * blk

    def body(j, carry):
        for u in range(nsub):
            row0 = pl.multiple_of(j * bq + u * blk, blk)
            for sb in range(nsub):
                for mp in range(2):
                    chain_update(sb, mp, row0, blk, False)
        return carry

    lax.fori_loop(0, i, body, 0)

    base = pl.multiple_of(i * bq, bq)
    for sb in range(nsub):
        done, visible = 0, (sb + 1) * blk
        while done < visible:
            width = min(kw, visible - done)
            for mp in range(2):
                chain_update(sb, mp, base + done, width, done + width == visible)
            done += width

    lam = (jnp.exp(jnp.sum(lq1_ref[...] * lk1_ref[...], keepdims=True))
           - jnp.exp(jnp.sum(lq2_ref[...] * lk2_ref[...], keepdims=True)) + lambda_init)
    for sb in range(nsub):
        a0, a1 = acc_ref[2 * sb], acc_ref[2 * sb + 1]
        o = a0[:, 0:hw] / a0[:, hw:hw + 1] - lam * (a1[:, 0:hw] / a1[:, hw:hw + 1])
        o = o * lax.rsqrt(jnp.mean(o * o, axis=-1, keepdims=True) + EPS)
        o_ref[sb * blk:(sb + 1) * blk, :] = (o * sg_ref[...] * (1.0 - lambda_init)).astype(o_ref.dtype)


def _attention(q, k, v, lq1, lk1, lq2, lk2, subln_g, lambda_init, blk, nsub):
    H, T, hw = q.shape
    bq = blk * nsub
    vec = pl.BlockSpec((1, ATTN_HEAD_DIM), lambda h, i: (0, 0))
    kernel = functools.partial(_attn_kernel, blk=blk, nsub=nsub, lambda_init=lambda_init)
    return pl.pallas_call(
        kernel,
        grid=(H, T // bq),
        in_specs=[vec, vec, vec, vec,
                  pl.BlockSpec((1, hw), lambda h, i: (0, 0)),
                  pl.BlockSpec((None, bq, hw), lambda h, i: (h, i, 0)),
                  pl.BlockSpec((None, T, hw), lambda h, i: (h, 0, 0), pipeline_mode=pl.Buffered(1)),
                  pl.BlockSpec((None, T, 2 * hw), lambda h, i: (h, 0, 0), pipeline_mode=pl.Buffered(1))],
        out_specs=pl.BlockSpec((bq, hw), lambda h, i: (i, h)),
        out_shape=jax.ShapeDtypeStruct((T, H * hw), _BF16),
        scratch_shapes=[pltpu.VMEM((2 * nsub, blk, LANES), _F32),
                        pltpu.VMEM((2 * nsub, blk, 2 * hw), _F32)],
        compiler_params=_cparams("parallel", "parallel"),
        name="diff_attention",
    )(lq1, lk1, lq2, lk2, subln_g, q, k, v)


def _shift_rows(x, s, fill):
    rolled = pltpu.roll(x, s, axis=0)
    row = lax.broadcasted_iota(jnp.int32, x.shape, 0)
    return jnp.where(row >= s, rolled, fill)


def _rglru_tile(x, gate, cw_ref, cb_ref, wg_ref, ba_ref, bx_ref, lam_ref, xbuf_ref, hprev_ref):
    halo = SUBLANES
    tl, W = x.shape
    xbuf_ref[halo:halo + tl, :] = x
    y = cb_ref[...] + x * cw_ref[CONV_WIDTH - 1:CONV_WIDTH, :]
    for t in range(CONV_WIDTH - 1):
        back = CONV_WIDTH - 1 - t
        y = y + xbuf_ref[halo - back:halo - back + tl, :] * cw_ref[t:t + 1, :]
    xbuf_ref[0:halo, :] = x[tl - halo:, :]

    gates = jnp.dot(y.astype(_BF16), wg_ref[...], preferred_element_type=_F32)
    r = jax.nn.sigmoid(gates[:, :W] + ba_ref[...])
    ig = jax.nn.sigmoid(gates[:, W:] + bx_ref[...])
    log_a = -LRU_C * r * jax.nn.softplus(-lam_ref[...])
    a = jnp.exp(log_a)
    one_minus_a2 = -jnp.tanh(log_a) * (1.0 + a * a)
    b = jnp.sqrt(jnp.maximum(one_minus_a2, 1e-12)) * (ig * y)

    s = 1
    while s < tl:
        a_sh = _shift_rows(a, s, 1.0)
        b_sh = _shift_rows(b, s, 0.0)
        b = a * b_sh + b
        a = a * a_sh
        s *= 2
    h = b + a * hprev_ref[0:1, :]
    hprev_ref[...] = jnp.broadcast_to(h[tl - 1:tl, :], hprev_ref.shape)
    return h * jax.nn.gelu(gate)


def _oddeven_mergesort_pairs(n):
    pairs = []
    p = 1
    while p < n:
        k = p
        while k >= 1:
            for j in range(k % p, n - k, 2 * k):
                for i in range(min(k, n - j - k)):
                    if (i + j) // (2 * p) == (i + j + k) // (2 * p):
                        pairs.append((i + j, i + j + k))
            k //= 2
        p *= 2
    return pairs


_SORT16 = _oddeven_mergesort_pairs(PEER_TOPK)


def _sort_desc(xs):
    xs = list(xs)
    for lo, hi in _SORT16:
        a, b = xs[lo], xs[hi]
        xs[lo], xs[hi] = jnp.maximum(a, b), jnp.minimum(a, b)
    return xs


def _top_of_union(a_sorted, b_sorted):
    n = len(a_sorted)
    return [jnp.maximum(a_sorted[i], b_sorted[n - 1 - i]) for i in range(n)]


def _bitonic_merge_desc(xs):
    xs = list(xs)
    n = len(xs)
    d = n // 2
    while d >= 1:
        for i in range(n):
            if i & d == 0:
                a, b = xs[i], xs[i + d]
                xs[i], xs[i + d] = jnp.maximum(a, b), jnp.minimum(a, b)
        d //= 2
    return xs


def _prefix_count(pred, vals):
    assert len(vals) == PEER_TOPK == 16
    bits = []
    step = 8
    while step >= 1:
        cands = [vals[base + step - 1] for base in range(0, 16, 2 * step)]
        for b in reversed(bits):
            cands = [jnp.where(b, hi, lo) for lo, hi in zip(cands[0::2], cands[1::2])]
        bits.append(pred(cands[0]))
        step //= 2
    count = jnp.where(pred(vals[15]), 1.0, 0.0)
    for b, w in zip(bits, (8.0, 4.0, 2.0, 1.0)):
        count = count + jnp.where(b, w, 0.0)
    return count


def _pair_threshold(t1, t2):
    k = PEER_TOPK
    rows = [[t1[a] + t2[b] for b in range(k // (a + 1))] for a in range(k)]
    neg = jnp.full_like(t1[0], NEG_INF)
    g1 = _sort_desc(rows[1] + rows[2] + rows[4])
    g2 = _sort_desc(rows[3] + rows[5] + rows[6] + rows[7]
                    + [rows[a][0] for a in range(8, 14)])
    g3 = _sort_desc([rows[14][0], rows[15][0]] + [neg] * (k - 2))
    top = _bitonic_merge_desc(_top_of_union(rows[0], g1))
    top = _bitonic_merge_desc(_top_of_union(top, g2))
    top = _top_of_union(top, g3)
    return top, rows[0][0]


def _mid_kernel(attn_ref, lru_ref, x_ref, wo_ref, g2_ref, wqt_ref, k1_ref, k2_ref,
                h_ref, xn_ref, c1_ref, e1_ref, r2_ref, e2_ref, qt_ref, sorted_ref, top2_ref, stat_ref):
    aw = attn_ref.shape[-1]
    h = (x_ref[...]
         + jnp.dot(attn_ref[...], wo_ref[0:aw, :], preferred_element_type=_F32)
         + jnp.dot(lru_ref[...], wo_ref[aw:, :], preferred_element_type=_F32))
    h_ref[...] = h
    xn = _rms(h, g2_ref[...]).astype(_BF16)
    xn_ref[...] = xn

    nslab = PEER_N_KEYS // SUBLANES
    tm = qt_ref.shape[-1]
    qw = 2 * PEER_HALF
    assert PEER_HEADS == SUBLANES, "phase 2 puts one head on each sublane"

    def scores(qt_head):
        q1 = qt_head[0:PEER_HALF, :].astype(_BF16)
        q2 = qt_head[PEER_HALF:qw, :].astype(_BF16)
        return (jnp.dot(k1_ref[...], q1, preferred_element_type=_F32),
                jnp.dot(k2_ref[...], q2, preferred_element_type=_F32))

    qt_ref[...] = lax.dot_general(wqt_ref[...], xn, _NT_DIMS, preferred_element_type=_F32)

    def head_queries(hd):
        return qt_ref[pl.ds(pl.multiple_of(hd * qw, qw), qw), :]

    def sort_head(hd, carry):
        for side, s in enumerate(scores(head_queries(hd))):
            srt = _sort_desc([s[j * SUBLANES:(j + 1) * SUBLANES, :] for j in range(nslab)])
            for b in range(PEER_TOPK):
                for sl in range(SUBLANES):
                    sorted_ref[side, b, sl, pl.ds(hd, 1), :] = srt[b][sl:sl + 1, :]
        return carry

    lax.fori_loop(0, PEER_HEADS, sort_head, 0)

    def merged_top16(side):
        lists = [[sorted_ref[side, b, sl] for b in range(PEER_TOPK)] for sl in range(SUBLANES)]
        while len(lists) > 1:
            lists = [_bitonic_merge_desc(_top_of_union(a, b)) for a, b in zip(lists[0::2], lists[1::2])]
        return lists[0]

    t1, t2 = merged_top16(0), merged_top16(1)
    top, m = _pair_threshold(t1, t2)
    z = functools.reduce(jnp.add, [jnp.exp(t - m) for t in top])
    for b in range(PEER_TOPK):
        top2_ref[b] = t2[b]
    stat_ref[0] = functools.reduce(jnp.minimum, top)
    stat_ref[1] = t1[0]
    stat_ref[2] = 1.0 / z

    def tables(hd, carry):
        def head_row(ref, k):
            return jnp.broadcast_to(ref[k, pl.ds(hd, 1), :], (SUBLANES, tm))

        s1, s2 = scores(head_queries(hd))
        t2h = [head_row(top2_ref, b) for b in range(PEER_TOPK)]
        tau, m1, inv_z = head_row(stat_ref, 0), head_row(stat_ref, 1), head_row(stat_ref, 2)
        ranks, e2s = [], []
        for j in range(nslab):
            rows = slice(j * SUBLANES, (j + 1) * SUBLANES)
            s1j, s2j = s1[rows, :], s2[rows, :]
            cnt = _prefix_count(lambda v: s1j + v >= tau, t2h)
            rank = _prefix_count(lambda v: v > s2j, t2h)
            c1_ref[hd, rows, :] = cnt
            e1_ref[hd, rows, :] = jnp.exp(s1j - m1)
            ranks.append(rank)
            e2s.append(jnp.exp(s2j - t2h[0]) * inv_z)
        r2_ref[hd] = jnp.concatenate(ranks, axis=0).astype(_BF16)
        e2_ref[hd] = jnp.concatenate(e2s, axis=0).astype(_BF16)
        return carry

    lax.fori_loop(0, PEER_HEADS, tables, 0)


def _mid(attn, lru, x2, wo_bf16, g2, wqt_bf16, k1_bf16, k2_bf16, tm):
    T, D = x2.shape
    aw, lw = attn.shape[1], lru.shape[1]
    tab_f32 = jax.ShapeDtypeStruct((PEER_HEADS, PEER_N_KEYS, T), _F32)
    tab_bf16 = jax.ShapeDtypeStruct((PEER_HEADS, PEER_N_KEYS, T), _BF16)
    tab_spec = pl.BlockSpec((PEER_HEADS, PEER_N_KEYS, tm), lambda i: (0, 0, i))
    const = lambda a: pl.BlockSpec(a.shape, lambda i: (0,) * a.ndim)
    return pl.pallas_call(
        _mid_kernel,
        grid=(T // tm,),
        in_specs=[pl.BlockSpec((tm, aw), lambda i: (i, 0)),
                  pl.BlockSpec((tm, lw), lambda i: (i, 0)),
                  pl.BlockSpec((tm, D), lambda i: (i, 0)),
                  const(wo_bf16), const(g2), const(wqt_bf16), const(k1_bf16), const(k2_bf16)],
        out_specs=[pl.BlockSpec((tm, D), lambda i: (i, 0)),
                   pl.BlockSpec((tm, D), lambda i: (i, 0)),
                   tab_spec, tab_spec, tab_spec, tab_spec],
        out_shape=[jax.ShapeDtypeStruct((T, D), _F32),
                   jax.ShapeDtypeStruct((T, D), _BF16),
                   tab_f32, tab_f32, tab_bf16, tab_bf16],
        scratch_shapes=[pltpu.VMEM((wqt_bf16.shape[0], tm), _F32),
                        pltpu.VMEM((2, PEER_TOPK, SUBLANES, PEER_HEADS, tm), _F32),
                        pltpu.VMEM((PEER_TOPK, PEER_HEADS, tm), _F32),
                        pltpu.VMEM((3, PEER_HEADS, tm), _F32)],
        compiler_params=_cparams("parallel"),
        name="mid_proj_topk",
    )(attn, lru, x2, wo_bf16, g2, wqt_bf16, k1_bf16, k2_bf16)


def _gelu_tanh(x):
    c1 = 2.0 * math.sqrt(2.0 / math.pi)
    c2 = c1 * 0.044715
    return x / (1.0 + jnp.exp(x * (-c1 - c2 * (x * x))))


def _peer_kernel(xn_ref, down_ref, upt_ref, c1_ref, e1_ref, r2_ref, e2_ref, h_ref, gf_ref,
                 o_ref, acc_ref, hid_ref, *, te):
    j = pl.program_id(1)
    nk = PEER_N_KEYS
    tm = xn_ref.shape[0]
    W = BF16_VREG_LANES
    R = 64
    nb = R // SUBLANES
    mb = 2 * nk

    @pl.when(j == 0)
    def _():
        acc_ref[...] = jnp.zeros(acc_ref.shape, _F32)

    row0 = pl.multiple_of(j * (te // nk), SUBLANES)
    zero = jnp.zeros((nb, SUBLANES, W), _BF16)
    xn = xn_ref[...]
    for m in range(te // mb):
        act = lax.dot_general(down_ref[m * mb:(m + 1) * mb, :], xn, _NT_DIMS,
                              preferred_element_type=_F32)
        for c in range(tm // W):
            lanes = slice(c * W, (c + 1) * W)
            for a in range(mb // nk):
                i1 = m * (mb // nk) + a
                cnts, e1s = [], []
                for hd in range(PEER_HEADS):
                    grp = pl.ds(row0 + (i1 // SUBLANES) * SUBLANES, SUBLANES)
                    c1_row = c1_ref[hd, grp, lanes][i1 % SUBLANES:i1 % SUBLANES + 1, :]
                    e1_row = e1_ref[hd, grp, lanes][i1 % SUBLANES:i1 % SUBLANES + 1, :]
                    cnts.append(jnp.broadcast_to(c1_row, (SUBLANES, W)).astype(_BF16)[None])
                    e1s.append(jnp.broadcast_to(e1_row, (SUBLANES, W)).astype(_BF16)[None])
                for rb in range(nk // R):
                    gate = zero
                    for hd in range(PEER_HEADS):
                        r2 = r2_ref[hd, rb * R:(rb + 1) * R, lanes].reshape(nb, SUBLANES, W)
                        e2 = e2_ref[hd, rb * R:(rb + 1) * R, lanes].reshape(nb, SUBLANES, W)
                        gate = gate + jnp.where(r2 < cnts[hd], e2, zero) * e1s[hd]
                    g = _gelu_tanh(act[a * nk + rb * R:a * nk + (rb + 1) * R, lanes].astype(_BF16))
                    rows = slice(i1 * nk + rb * R, i1 * nk + (rb + 1) * R)
                    hid_ref[rows, lanes] = (g.reshape(nb, SUBLANES, W) * gate).reshape(R, W)
    acc_ref[...] += jnp.dot(upt_ref[...], hid_ref[...], preferred_element_type=_F32)

    @pl.when(j == pl.num_programs(1) - 1)
    def _():
        o_ref[...] = _rms(h_ref[...] + jnp.transpose(acc_ref[...]), gf_ref[...])


def _peer(xn, down_bf16, upt_bf16, c1, e1, r2, e2, h, gf, tm, te):
    T, D = xn.shape
    E = down_bf16.shape[0]
    nk = PEER_N_KEYS
    assert te % (nk * SUBLANES) == 0, "an expert tile spans whole sublane groups of first-key rows"
    tab_spec = pl.BlockSpec((PEER_HEADS, nk, tm), lambda i, j: (0, 0, i))
    return pl.pallas_call(
        functools.partial(_peer_kernel, te=te),
        grid=(T // tm, E // te),
        in_specs=[pl.BlockSpec((tm, D), lambda i, j: (i, 0)),
                  pl.BlockSpec((te, D), lambda i, j: (j, 0)),
                  pl.BlockSpec((D, te), lambda i, j: (0, j)),
                  tab_spec, tab_spec, tab_spec, tab_spec,
                  pl.BlockSpec((tm, D), lambda i, j: (i, 0)),
                  pl.BlockSpec((1, D), lambda i, j: (0, 0))],
        out_specs=pl.BlockSpec((tm, D), lambda i, j: (i, 0)),
        out_shape=jax.ShapeDtypeStruct((T, D), _F32),
        scratch_shapes=[pltpu.VMEM((D, tm), _F32), pltpu.VMEM((te, tm), _BF16)],
        compiler_params=_cparams("parallel", "arbitrary"),
        name="peer_experts",
    )(xn, down_bf16, upt_bf16, c1, e1, r2, e2, h, gf)


def _block_diag(w):
    n, d, _ = w.shape
    eye = jnp.eye(n, dtype=w.dtype)
    return (eye[:, None, :, None] * w[:, :, None, :]).reshape(n * d, n * d)


def _tile(n, want):
    t = min(n, want)
    assert n % t == 0, (n, want)
    return t


def kernel(x, norm1_g, w_in, lambda_q1, lambda_k1, lambda_q2, lambda_k2, subln_g, conv_w, conv_b,
           w_rec_gate, b_rec_gate, w_in_gate, b_in_gate, lru_lambda, w_out, norm2_g, w_query,
           sub_keys_1, sub_keys_2, expert_down, expert_up, norm_f_g):
    B, S, D = x.shape
    T = B * S
    assert B == 1, "sequence mixing kernels assume a single sequence"
    assert w_in.shape[0] == 1, "the final norm is fused into the (single) layer's PEER kernel"
    l = 0
    h = x.reshape(T, D)
    row = lambda v: v.reshape(1, -1)
    lambda_init = 0.8 - 0.6 * math.exp(-0.3 * l)
    w_gates = jnp.concatenate([_block_diag(w_rec_gate[l]), _block_diag(w_in_gate[l])],
                              axis=1).astype(_BF16)
    q, k, v, lru = _in_proj(h, row(norm1_g[l]), w_in[l].astype(_BF16), conv_w[l], row(conv_b[l]),
                            w_gates, row(b_rec_gate[l]), row(b_in_gate[l]), row(lru_lambda[l]),
                            _tile(T, 512))
    attn = _attention(q, k, v, row(lambda_q1[l]), row(lambda_k1[l]), row(lambda_q2[l]),
                      row(lambda_k2[l]), row(subln_g[l]), lambda_init, _tile(T, 512), 4)
    h_mid, xnt, c1, e1, r2, e2 = _mid(
        attn, lru, h, w_out[l].astype(_BF16), row(norm2_g[l]),
        jnp.transpose(w_query[l]).astype(_BF16), sub_keys_1[l].astype(_BF16),
        sub_keys_2[l].astype(_BF16), _tile(T, 512))
    out = _peer(xnt, expert_down[l].astype(_BF16), jnp.transpose(expert_up[l]).astype(_BF16),
                c1, e1, r2, e2, h_mid, row(norm_f_g), _tile(T, 512), 2048)
    return out.reshape(B, S, D)
```

```python
import functools
import math

import jax
import jax.numpy as jnp
from jax import lax
from jax.experimental import pallas as pl
from jax.experimental.pallas import tpu as pltpu

CHUNK = 64
ATTN_HEAD_DIM = 64
ATTN_HEADS = 4
ATTN_WIDTH = 2 * ATTN_HEAD_DIM * ATTN_HEADS
LRU_BLOCKS = 8
CONV_WIDTH = 4
LRU_C = 8.0
PEER_HEADS = 8
PEER_N_KEYS = 128
PEER_HALF = 128
PEER_TOPK = 16
EPS = 1e-6
NEG_INF = -1e30

LANES = 128
SUBLANES = 8
BF16_VREG_LANES = 256
VMEM_LIMIT_BYTES = 48 * 1024 * 1024

_F32 = jnp.float32
_BF16 = jnp.bfloat16
_NT_DIMS = (((1,), (1,)), ((), ()))


def _cparams(*sem):
    return pltpu.CompilerParams(dimension_semantics=sem, vmem_limit_bytes=VMEM_LIMIT_BYTES)


def _rms(x, g):
    return x * lax.rsqrt(jnp.mean(x * x, axis=-1, keepdims=True) + EPS) * g


def _in_proj_kernel(x_ref, g_ref, w_ref, cw_ref, cb_ref, wg_ref, ba_ref, bx_ref, lam_ref,
                    q_ref, k_ref, v_ref, lru_ref, xbuf_ref, hprev_ref):
    n = _rms(x_ref[...], g_ref[...]).astype(_BF16)
    hw = 2 * ATTN_HEAD_DIM

    def proj(c0, width):
        return jnp.dot(n, w_ref[:, c0:c0 + width], preferred_element_type=_F32)

    scale = ATTN_HEAD_DIM ** -0.5 * math.log2(math.e)
    ones_col = (lax.broadcasted_iota(jnp.int32, (x_ref.shape[0], hw), 1) == 0).astype(_BF16)
    q_all = proj(0, ATTN_WIDTH) * scale
    k_all = proj(ATTN_WIDTH, ATTN_WIDTH)
    v_all = proj(2 * ATTN_WIDTH, ATTN_WIDTH)
    for h in range(ATTN_HEADS):
        cols = slice(h * hw, (h + 1) * hw)
        q_ref[h] = q_all[:, cols].astype(_BF16)
        k_ref[h] = k_all[:, cols].astype(_BF16)
        v_ref[h, :, 0:hw] = v_all[:, cols].astype(_BF16)
        v_ref[h, :, hw:2 * hw] = ones_col
    lru_w = lru_ref.shape[-1]
    xr = proj(3 * ATTN_WIDTH, lru_w)
    gate = proj(3 * ATTN_WIDTH + lru_w, lru_w)
    @pl.when(pl.program_id(0) == 0)
    def _():
        xbuf_ref[0:SUBLANES, :] = jnp.zeros((SUBLANES, lru_w), _F32)
        hprev_ref[...] = jnp.zeros(hprev_ref.shape, _F32)

    tl = xbuf_ref.shape[0] - SUBLANES
    for r0 in range(0, x_ref.shape[0], tl):
        out = _rglru_tile(xr[r0:r0 + tl], gate[r0:r0 + tl], cw_ref, cb_ref, wg_ref, ba_ref, bx_ref,
                          lam_ref, xbuf_ref, hprev_ref)
        lru_ref[r0:r0 + tl, :] = out.astype(lru_ref.dtype)


def _in_proj(x2, g, w_bf16, conv_w, conv_b, wg_bf16, b_a, b_x, lru_lambda, tm):
    T, D = x2.shape
    lru_w = (w_bf16.shape[1] - 3 * ATTN_WIDTH) // 2
    hw = 2 * ATTN_HEAD_DIM
    head_shape = jax.ShapeDtypeStruct((ATTN_HEADS, T, hw), _BF16)
    head_spec = pl.BlockSpec((ATTN_HEADS, tm, hw), lambda i: (0, i, 0))
    v_shape = jax.ShapeDtypeStruct((ATTN_HEADS, T, 2 * hw), _BF16)
    v_spec = pl.BlockSpec((ATTN_HEADS, tm, 2 * hw), lambda i: (0, i, 0))
    const = lambda a: pl.BlockSpec(a.shape, lambda i: (0,) * a.ndim)
    return pl.pallas_call(
        _in_proj_kernel,
        grid=(T // tm,),
        in_specs=[pl.BlockSpec((tm, D), lambda i: (i, 0)),
                  const(g), const(w_bf16), const(conv_w), const(conv_b), const(wg_bf16),
                  const(b_a), const(b_x), const(lru_lambda)],
        out_specs=[head_spec, head_spec, v_spec, pl.BlockSpec((tm, lru_w), lambda i: (i, 0))],
        out_shape=[head_shape, head_shape, v_shape, jax.ShapeDtypeStruct((T, lru_w), _BF16)],
        scratch_shapes=[pltpu.VMEM((tm // 2 + SUBLANES, lru_w), _F32),
                        pltpu.VMEM((SUBLANES, lru_w), _F32)],
        compiler_params=_cparams("arbitrary"),
        name="in_proj_rglru",
    )(x2, g, w_bf16, conv_w, conv_b, wg_bf16, b_a, b_x, lru_lambda)


def _attn_kernel(lq1_ref, lk1_ref, lq2_ref, lk2_ref, sg_ref, q_ref, k_ref, v_ref, o_ref,
                 m_ref, acc_ref, *, blk, nsub, lambda_init):
    i = pl.program_id(1)
    hd = ATTN_HEAD_DIM
    hw = 2 * hd
    lane = lax.broadcasted_iota(jnp.int32, (blk, hw), 1)
    q_maps = []
    for sb in range(nsub):
        q = q_ref[sb * blk:(sb + 1) * blk, :]
        zero = jnp.zeros_like(q)
        q_maps.append((jnp.where(lane < hd, q, zero), jnp.where(lane >= hd, q, zero)))

    m_ref[...] = jnp.full(m_ref.shape, NEG_INF, _F32)
    acc_ref[...] = jnp.zeros(acc_ref.shape, _F32)

    def chain_update(sb, mp, row0, width, ends_on_diagonal):
        kj = k_ref[pl.ds(row0, width), :]
        vj = v_ref[pl.ds(row0, width), :]
        s = lax.dot_general(q_maps[sb][mp], kj, _NT_DIMS, preferred_element_type=_F32)
        if ends_on_diagonal:
            qc = lax.broadcasted_iota(jnp.int32, (blk, width), 0) // CHUNK + (width - blk) // CHUNK
            kc = lax.broadcasted_iota(jnp.int32, (blk, width), 1) // CHUNK
            s = jnp.where(qc >= kc, s, NEG_INF)
        idx = 2 * sb + mp
        m_prev = m_ref[idx]
        m_next = jnp.maximum(m_prev, jnp.max(s, axis=1, keepdims=True))
        alpha = jnp.exp2(m_prev - m_next)
        e = jnp.exp2(s - jnp.tile(m_next, (1, width // LANES)))
        m_ref[idx] = m_next
        pv = jnp.dot(e.astype(_BF16), vj, preferred_element_type=_F32)
        acc_ref[idx] = jnp.tile(alpha, (1, 2 * hw // LANES)) * acc_ref[idx] + pv

    bq = nsub * blk
    kw = 2 * blk

    def body(j, carry):
        for u in range(nsub):
            row0 = pl.multiple_of(j * bq + u * blk, blk)
            for sb in range(nsub):
                for mp in range(2):
                    chain_update(sb, mp, row0, blk, False)
        return carry

    lax.fori_loop(0, i, body, 0)

    base = pl.multiple_of(i * bq, bq)
    for sb in range(nsub):
        done, visible = 0, (sb + 1) * blk
        while done < visible:
            width = min(kw, visible - done)
            for mp in range(2):
                chain_update(sb, mp, base + done, width, done + width == visible)
            done += width

    lam = (jnp.exp(jnp.sum(lq1_ref[...] * lk1_ref[...], keepdims=True))
           - jnp.exp(jnp.sum(lq2_ref[...] * lk2_ref[...], keepdims=True)) + lambda_init)
    for sb in range(nsub):
        a0, a1 = acc_ref[2 * sb], acc_ref[2 * sb + 1]
        o = a0[:, 0:hw] / a0[:, hw:hw + 1] - lam * (a1[:, 0:hw] / a1[:, hw:hw + 1])
        o = o * lax.rsqrt(jnp.mean(o * o, axis=-1, keepdims=True) + EPS)
        o_ref[sb * blk:(sb + 1) * blk, :] = (o * sg_ref[...] * (1.0 - lambda_init)).astype(o_ref.dtype)


def _attention(q, k, v, lq1, lk1, lq2, lk2, subln_g, lambda_init, blk, nsub):
    H, T, hw = q.shape
    bq = blk * nsub
    vec = pl.BlockSpec((1, ATTN_HEAD_DIM), lambda h, i: (0, 0))
    kernel = functools.partial(_attn_kernel, blk=blk, nsub=nsub, lambda_init=lambda_init)
    return pl.pallas_call(
        kernel,
        grid=(H, T // bq),
        in_specs=[vec, vec, vec, vec,
                  pl.BlockSpec((1, hw), lambda h, i: (0, 0)),
                  pl.BlockSpec((None, bq, hw), lambda h, i: (h, i, 0)),
                  pl.BlockSpec((None, T, hw), lambda h, i: (h, 0, 0), pipeline_mode=pl.Buffered(1)),
                  pl.BlockSpec((None, T, 2 * hw), lambda h, i: (h, 0, 0), pipeline_mode=pl.Buffered(1))],
        out_specs=pl.BlockSpec((bq, hw), lambda h, i: (i, h)),
        out_shape=jax.ShapeDtypeStruct((T, H * hw), _BF16),
        scratch_shapes=[pltpu.VMEM((2 * nsub, blk, LANES), _F32),
                        pltpu.VMEM((2 * nsub, blk, 2 * hw), _F32)],
        compiler_params=_cparams("parallel", "parallel"),
        name="diff_attention",
    )(lq1, lk1, lq2, lk2, subln_g, q, k, v)


def _shift_rows(x, s, fill):
    rolled = pltpu.roll(x, s, axis=0)
    row = lax.broadcasted_iota(jnp.int32, x.shape, 0)
    return jnp.where(row >= s, rolled, fill)


def _rglru_tile(x, gate, cw_ref, cb_ref, wg_ref, ba_ref, bx_ref, lam_ref, xbuf_ref, hprev_ref):
    halo = SUBLANES
    tl, W = x.shape
    xbuf_ref[halo:halo + tl, :] = x
    y = cb_ref[...] + x * cw_ref[CONV_WIDTH - 1:CONV_WIDTH, :]
    for t in range(CONV_WIDTH - 1):
        back = CONV_WIDTH - 1 - t
        y = y + xbuf_ref[halo - back:halo - back + tl, :] * cw_ref[t:t + 1, :]
    xbuf_ref[0:halo, :] = x[tl - halo:, :]

    gates = jnp.dot(y.astype(_BF16), wg_ref[...], preferred_element_type=_F32)
    r = jax.nn.sigmoid(gates[:, :W] + ba_ref[...])
    ig = jax.nn.sigmoid(gates[:, W:] + bx_ref[...])
    log_a = -LRU_C * r * jax.nn.softplus(-lam_ref[...])
    a = jnp.exp(log_a)
    one_minus_a2 = -jnp.tanh(log_a) * (1.0 + a * a)
    b = jnp.sqrt(jnp.maximum(one_minus_a2, 1e-12)) * (ig * y)

    s = 1
    while s < tl:
        a_sh = _shift_rows(a, s, 1.0)
        b_sh = _shift_rows(b, s, 0.0)
        b = a * b_sh + b
        a = a * a_sh
        s *= 2
    h = b + a * hprev_ref[0:1, :]
    hprev_ref[...] = jnp.broadcast_to(h[tl - 1:tl, :], hprev_ref.shape)
    return h * jax.nn.gelu(gate)


def _oddeven_mergesort_pairs(n):
    pairs = []
    p = 1
    while p < n:
        k = p
        while k >= 1:
            for j in range(k % p, n - k, 2 * k):
                for i in range(min(k, n - j - k)):
                    if (i + j) // (2 * p) == (i + j + k) // (2 * p):
                        pairs.append((i + j, i + j + k))
            k //= 2
        p *= 2
    return pairs


_SORT16 = _oddeven_mergesort_pairs(PEER_TOPK)


def _sort_desc(xs):
    xs = list(xs)
    for lo, hi in _SORT16:
        a, b = xs[lo], xs[hi]
        xs[lo], xs[hi] = jnp.maximum(a, b), jnp.minimum(a, b)
    return xs


def _top_of_union(a_sorted, b_sorted):
    n = len(a_sorted)
    return [jnp.maximum(a_sorted[i], b_sorted[n - 1 - i]) for i in range(n)]


def _bitonic_merge_desc(xs):
    xs = list(xs)
    n = len(xs)
    d = n // 2
    while d >= 1:
        for i in range(n):
            if i & d == 0:
                a, b = xs[i], xs[i + d]
                xs[i], xs[i + d] = jnp.maximum(a, b), jnp.minimum(a, b)
        d //= 2
    return xs


def _prefix_count(pred, vals):
    assert len(vals) == PEER_TOPK == 16
    bits = []
    step = 8
    while step >= 1:
        cands = [vals[base + step - 1] for base in range(0, 16, 2 * step)]
        for b in reversed(bits):
            cands = [jnp.where(b, hi, lo) for lo, hi in zip(cands[0::2], cands[1::2])]
        bits.append(pred(cands[0]))
        step //= 2
    count = jnp.where(pred(vals[15]), 1.0, 0.0)
    for b, w in zip(bits, (8.0, 4.0, 2.0, 1.0)):
        count = count + jnp.where(b, w, 0.0)
    return count


def _pair_threshold(t1, t2):
    k = PEER_TOPK
    rows = [[t1[a] + t2[b] for b in range(k // (a + 1))] for a in range(k)]
    neg = jnp.full_like(t1[0], NEG_INF)
    g1 = _sort_desc(rows[1] + rows[2] + rows[4])
    g2 = _sort_desc(rows[3] + rows[5] + rows[6] + rows[7]
                    + [rows[a][0] for a in range(8, 14)])
    g3 = _sort_desc([rows[14][0], rows[15][0]] + [neg] * (k - 2))
    top = _bitonic_merge_desc(_top_of_union(rows[0], g1))
    top = _bitonic_merge_desc(_top_of_union(top, g2))
    top = _top_of_union(top, g3)
    return top, rows[0][0]


def _mid_kernel(attn_ref, lru_ref, x_ref, wo_ref, g2_ref, wqt_ref, k1_ref, k2_ref,
                h_ref, xn_ref, c1_ref, e1_ref, r2_ref, e2_ref, qt_ref, sorted_ref, top2_ref, stat_ref):
    aw = attn_ref.shape[-1]
    h = (x_ref[...]
         + jnp.dot(attn_ref[...], wo_ref[0:aw, :], preferred_element_type=_F32)
         + jnp.dot(lru_ref[...], wo_ref[aw:, :], preferred_element_type=_F32))
    h_ref[...] = h
    xn = _rms(h, g2_ref[...]).astype(_BF16)
    xn_ref[...] = xn

    nslab = PEER_N_KEYS // SUBLANES
    tm = qt_ref.shape[-1]
    qw = 2 * PEER_HALF
    assert PEER_HEADS == SUBLANES, "phase 2 puts one head on each sublane"

    def scores(qt_head):
        q1 = qt_head[0:PEER_HALF, :].astype(_BF16)
        q2 = qt_head[PEER_HALF:qw, :].astype(_BF16)
        return (jnp.dot(k1_ref[...], q1, preferred_element_type=_F32),
                jnp.dot(k2_ref[...], q2, preferred_element_type=_F32))

    qt_ref[...] = lax.dot_general(wqt_ref[...], xn, _NT_DIMS, preferred_element_type=_F32)

    def head_queries(hd):
        return qt_ref[pl.ds(pl.multiple_of(hd * qw, qw), qw), :]

    def sort_head(hd, carry):
        for side, s in enumerate(scores(head_queries(hd))):
            srt = _sort_desc([s[j * SUBLANES:(j + 1) * SUBLANES, :] for j in range(nslab)])
            for b in range(PEER_TOPK):
                for sl in range(SUBLANES):
                    sorted_ref[side, b, sl, pl.ds(hd, 1), :] = srt[b][sl:sl + 1, :]
        return carry

    lax.fori_loop(0, PEER_HEADS, sort_head, 0)

    def merged_top16(side):
        lists = [[sorted_ref[side, b, sl] for b in range(PEER_TOPK)] for sl in range(SUBLANES)]
        while len(lists) > 1:
            lists = [_bitonic_merge_desc(_top_of_union(a, b)) for a, b in zip(lists[0::2], lists[1::2])]
        return lists[0]

    t1, t2 = merged_top16(0), merged_top16(1)
    top, m = _pair_threshold(t1, t2)
    z = functools.reduce(jnp.add, [jnp.exp(t - m) for t in top])
    for b in range(PEER_TOPK):
        top2_ref[b] = t2[b]
    stat_ref[0] = functools.reduce(jnp.minimum, top)
    stat_ref[1] = t1[0]
    stat_ref[2] = 1.0 / z

    def tables(hd, carry):
        def head_row(ref, k):
            return jnp.broadcast_to(ref[k, pl.ds(hd, 1), :], (SUBLANES, tm))

        s1, s2 = scores(head_queries(hd))
        t2h = [head_row(top2_ref, b) for b in range(PEER_TOPK)]
        tau, m1, inv_z = head_row(stat_ref, 0), head_row(stat_ref, 1), head_row(stat_ref, 2)
        ranks, e2s = [], []
        for j in range(nslab):
            rows = slice(j * SUBLANES, (j + 1) * SUBLANES)
            s1j, s2j = s1[rows, :], s2[rows, :]
            cnt = _prefix_count(lambda v: s1j + v >= tau, t2h)
            rank = _prefix_count(lambda v: v > s2j, t2h)
            c1_ref[hd, rows, :] = cnt
            e1_ref[hd, rows, :] = jnp.exp(s1j - m1)
            ranks.append(rank)
            e2s.append(jnp.exp(s2j - t2h[0]) * inv_z)
        r2_ref[hd] = jnp.concatenate(ranks, axis=0).astype(_BF16)
        e2_ref[hd] = jnp.concatenate(e2s, axis=0).astype(_BF16)
        return carry

    lax.fori_loop(0, PEER_HEADS, tables, 0)


def _mid(attn, lru, x2, wo_bf16, g2, wqt_bf16, k1_bf16, k2_bf16, tm):
    T, D = x2.shape
    aw, lw = attn.shape[1], lru.shape[1]
    tab_f32 = jax.ShapeDtypeStruct((PEER_HEADS, PEER_N_KEYS, T), _F32)
    tab_bf16 = jax.ShapeDtypeStruct((PEER_HEADS, PEER_N_KEYS, T), _BF16)
    tab_spec = pl.BlockSpec((PEER_HEADS, PEER_N_KEYS, tm), lambda i: (0, 0, i))
    const = lambda a: pl.BlockSpec(a.shape, lambda i: (0,) * a.ndim)
    return pl.pallas_call(
        _mid_kernel,
        grid=(T // tm,),
        in_specs=[pl.BlockSpec((tm, aw), lambda i: (i, 0)),
                  pl.BlockSpec((tm, lw), lambda i: (i, 0)),
                  pl.BlockSpec((tm, D), lambda i: (i, 0)),
                  const(wo_bf16), const(g2), const(wqt_bf16), const(k1_bf16), const(k2_bf16)],
        out_specs=[pl.BlockSpec((tm, D), lambda i: (i, 0)),
                   pl.BlockSpec((tm, D), lambda i: (i, 0)),
                   tab_spec, tab_spec, tab_spec, tab_spec],
        out_shape=[jax.ShapeDtypeStruct((T, D), _F32),
                   jax.ShapeDtypeStruct((T, D), _BF16),
                   tab_f32, tab_f32, tab_bf16, tab_bf16],
        scratch_shapes=[pltpu.VMEM((wqt_bf16.shape[0], tm), _F32),
                        pltpu.VMEM((2, PEER_TOPK, SUBLANES, PEER_HEADS, tm), _F32),
                        pltpu.VMEM((PEER_TOPK, PEER_HEADS, tm), _F32),
                        pltpu.VMEM((3, PEER_HEADS, tm), _F32)],
        compiler_params=_cparams("parallel"),
        name="mid_proj_topk",
    )(attn, lru, x2, wo_bf16, g2, wqt_bf16, k1_bf16, k2_bf16)


def _gelu_tanh(x):
    c1 = 2.0 * math.sqrt(2.0 / math.pi)
    c2 = c1 * 0.044715
    return x / (1.0 + jnp.exp(x * (-c1 - c2 * (x * x))))


def _peer_kernel(xn_ref, down_ref, upt_ref, c1_ref, e1_ref, r2_ref, e2_ref, h_ref, gf_ref,
                 o_ref, acc_ref, hid_ref, *, te):
    j = pl.program_id(1)
    nk = PEER_N_KEYS
    tm = xn_ref.shape[0]
    W = BF16_VREG_LANES
    R = 64
    nb = R // SUBLANES
    mb = 2 * nk

    @pl.when(j == 0)
    def _():
        acc_ref[...] = jnp.zeros(acc_ref.shape, _F32)

    row0 = pl.multiple_of(j * (te // nk), SUBLANES)
    zero = jnp.zeros((nb, SUBLANES, W), _BF16)
    xn = xn_ref[...]
    for m in range(te // mb):
        act = lax.dot_general(down_ref[m * mb:(m + 1) * mb, :], xn, _NT_DIMS,
                              preferred_element_type=_F32)
        for c in range(tm // W):
            lanes = slice(c * W, (c + 1) * W)
            for a in range(mb // nk):
                i1 = m * (mb // nk) + a
                cnts, e1s = [], []
                for hd in range(PEER_HEADS):
                    grp = pl.ds(row0 + (i1 // SUBLANES) * SUBLANES, SUBLANES)
                    c1_row = c1_ref[hd, grp, lanes][i1 % SUBLANES:i1 % SUBLANES + 1, :]
                    e1_row = e1_ref[hd, grp, lanes][i1 % SUBLANES:i1 % SUBLANES + 1, :]
                    cnts.append(jnp.broadcast_to(c1_row, (SUBLANES, W)).astype(_BF16)[None])
                    e1s.append(jnp.broadcast_to(e1_row, (SUBLANES, W)).astype(_BF16)[None])
                for rb in range(nk // R):
                    gate = zero
                    for hd in range(PEER_HEADS):
                        r2 = r2_ref[hd, rb * R:(rb + 1) * R, lanes].reshape(nb, SUBLANES, W)
                        e2 = e2_ref[hd, rb * R:(rb + 1) * R, lanes].reshape(nb, SUBLANES, W)
                        gate = gate + jnp.where(r2 < cnts[hd], e2, zero) * e1s[hd]
                    g = _gelu_tanh(act[a * nk + rb * R:a * nk + (rb + 1) * R, lanes].astype(_BF16))
                    rows = slice(i1 * nk + rb * R, i1 * nk + (rb + 1) * R)
                    hid_ref[rows, lanes] = (g.reshape(nb, SUBLANES, W) * gate).reshape(R, W)
    acc_ref[...] += lax.dot_general(upt_ref[...], hid_ref[...], (((0,), (0,)), ((), ())),
                                    preferred_element_type=_F32)

    @pl.when(j == pl.num_programs(1) - 1)
    def _():
        o_ref[...] = _rms(h_ref[...] + jnp.transpose(acc_ref[...]), gf_ref[...])


def _peer(xn, down_bf16, upt_bf16, c1, e1, r2, e2, h, gf, tm, te):
    T, D = xn.shape
    E = down_bf16.shape[0]
    nk = PEER_N_KEYS
    assert te % (nk * SUBLANES) == 0, "an expert tile spans whole sublane groups of first-key rows"
    tab_spec = pl.BlockSpec((PEER_HEADS, nk, tm), lambda i, j: (0, 0, i))
    return pl.pallas_call(
        functools.partial(_peer_kernel, te=te),
        grid=(T // tm, E // te),
        in_specs=[pl.BlockSpec((tm, D), lambda i, j: (i, 0)),
                  pl.BlockSpec((te, D), lambda i, j: (j, 0)),
                  pl.BlockSpec((te, D), lambda i, j: (j, 0)),
                  tab_spec, tab_spec, tab_spec, tab_spec,
                  pl.BlockSpec((tm, D), lambda i, j: (i, 0)),
                  pl.BlockSpec((1, D), lambda i, j: (0, 0))],
        out_specs=pl.BlockSpec((tm, D), lambda i, j: (i, 0)),
        out_shape=jax.ShapeDtypeStruct((T, D), _F32),
        scratch_shapes=[pltpu.VMEM((D, tm), _F32), pltpu.VMEM((te, tm), _BF16)],
        compiler_params=_cparams("parallel", "arbitrary"),
        name="peer_experts",
    )(xn, down_bf16, upt_bf16, c1, e1, r2, e2, h, gf)


def _block_diag(w):
    n, d, _ = w.shape
    eye = jnp.eye(n, dtype=w.dtype)
    return (eye[:, None, :, None] * w[:, :, None, :]).reshape(n * d, n * d)


def _tile(n, want):
    t = min(n, want)
    assert n % t == 0, (n, want)
    return t


def kernel(x, norm1_g, w_in, lambda_q1, lambda_k1, lambda_q2, lambda_k2, subln_g, conv_w, conv_b,
           w_rec_gate, b_rec_gate, w_in_gate, b_in_gate, lru_lambda, w_out, norm2_g, w_query,
           sub_keys_1, sub_keys_2, expert_down, expert_up, norm_f_g):
    B, S, D = x.shape
    T = B * S
    assert B == 1, "sequence mixing kernels assume a single sequence"
    assert w_in.shape[0] == 1, "the final norm is fused into the (single) layer's PEER kernel"
    l = 0
    h = x.reshape(T, D)
    row = lambda v: v.reshape(1, -1)
    lambda_init = 0.8 - 0.6 * math.exp(-0.3 * l)
    w_gates = jnp.concatenate([_block_diag(w_rec_gate[l]), _block_diag(w_in_gate[l])],
                              axis=1).astype(_BF16)
    q, k, v, lru = _in_proj(h, row(norm1_g[l]), w_in[l].astype(_BF16), conv_w[l], row(conv_b[l]),
                            w_gates, row(b_rec_gate[l]), row(b_in_gate[l]), row(lru_lambda[l]),
                            _tile(T, 512))
    attn = _attention(q, k, v, row(lambda_q1[l]), row(lambda_k1[l]), row(lambda_q2[l]),
                      row(lambda_k2[l]), row(subln_g[l]), lambda_init, _tile(T, 512), 4)
    h_mid, xnt, c1, e1, r2, e2 = _mid(
        attn, lru, h, w_out[l].astype(_BF16), row(norm2_g[l]),
        jnp.transpose(w_query[l]).astype(_BF16), sub_keys_1[l].astype(_BF16),
        sub_keys_2[l].astype(_BF16), _tile(T, 512))
    out = _peer(xnt, expert_down[l].astype(_BF16), expert_up[l].astype(_BF16),
                c1, e1, r2, e2, h_mid, row(norm_f_g), _tile(T, 512), 2048)
    return out.reshape(B, S, D)
```
